```python
import math
import jax, jax.numpy as jnp
from jax import lax
import numpy as np

D_MODEL = 1024
BATCH = 2
SEQ = 8192
DEPTH = 2

N_META = 16
DN_HEAD_DIM = 128
DN_HEADS = D_MODEL // DN_HEAD_DIM
DN_KEY_DIM = DN_HEADS * DN_HEAD_DIM
DN_VAL_DIM = DN_HEADS * DN_HEAD_DIM
QKV_DIM = 2 * DN_KEY_DIM + DN_VAL_DIM
DN_CONV = 4
CHUNK = 64
LEAD_PAD = (-N_META) % CHUNK
POOL_GROUPS = 4
POOL_WINDOWS = (2, 4, 8, 16)
POOL_GROUP_DIM = D_MODEL // 8
POOL_WIDTH = POOL_GROUPS * POOL_GROUP_DIM
POOL_OUT_GROUP = D_MODEL // POOL_GROUPS
N_BRANCHES = 2
SPLIT_SIZES = (QKV_DIM, DN_VAL_DIM, DN_HEADS, DN_HEADS, POOL_WIDTH, N_BRANCHES * D_MODEL)
IN_DIM = QKV_DIM + DN_VAL_DIM + 2 * DN_HEADS + POOL_WIDTH + N_BRANCHES * D_MODEL
D_FF = 256 * ((8 * D_MODEL // 3 + 255) // 256)
FFN_CONV = 3
NORM_EPS = 1e-6

kernel_name = 'hybrid_gdn_pool_meta_block'


def rms_norm(x, gain):
    xf = x.astype(jnp.float32)
    y = xf * lax.rsqrt(jnp.mean(xf * xf, axis=-1, keepdims=True) + NORM_EPS) * gain.astype(jnp.float32)
    return y.astype(x.dtype)


def l2_normalize(x):
    xf = x.astype(jnp.float32)
    return xf * lax.rsqrt(jnp.sum(xf * xf, axis=-1, keepdims=True) + NORM_EPS)


def causal_depthwise_conv(x, w):
    k_width = w.shape[0]
    length = x.shape[1]
    xp = jnp.pad(x, ((0, 0), (k_width - 1, 0), (0, 0)))
    out = xp[:, 0:length] * w[0]
    for j in range(1, k_width):
        out = out + xp[:, j:j + length] * w[j]
    return out


def chunk_gated_delta_rule(q, k, v, g, beta):
    b, t, h, dk = k.shape
    dv = v.shape[-1]
    n = t // CHUNK

    def chunks(a):
        a = a.reshape((b, n, CHUNK, h) + a.shape[3:])
        return jnp.moveaxis(a, 3, 1)

    q = chunks(q) * (dk ** -0.5)
    k = chunks(k)
    v = chunks(v)
    beta = chunks(beta)
    g = lax.cumsum(chunks(g), axis=3)
    idx = jnp.arange(CHUNK)
    causal = idx[:, None] >= idx[None, :]
    strict = idx[:, None] > idx[None, :]
    decay = jnp.exp(jnp.where(causal, g[..., :, None] - g[..., None, :], -jnp.inf))
    kb = k * beta[..., None]
    lower = jnp.where(strict, jnp.einsum('bhncd,bhnsd->bhncs', kb, k) * decay, 0.0)
    eye = jnp.eye(CHUNK, dtype=lower.dtype)
    tinv = lax.linalg.triangular_solve(lower + eye, jnp.broadcast_to(eye, lower.shape),
                                       left_side=True, lower=True, unit_diagonal=True)
    u = jnp.einsum('bhncs,bhnsv->bhncv', tinv, v * beta[..., None])
    w = jnp.einsum('bhncs,bhnsd->bhncd', tinv, kb * jnp.exp(g)[..., None])
    qk = jnp.where(causal, jnp.einsum('bhncd,bhnsd->bhncs', q, k) * decay, 0.0)
    q_dec = q * jnp.exp(g)[..., None]
    k_dec = k * jnp.exp(g[..., -1:] - g)[..., None]
    g_tot = jnp.exp(g[..., -1])

    def step(state, inp):
        q_i, k_i, u_i, w_i, qk_i, gt_i = inp
        v_new = u_i - jnp.einsum('bhck,bhkv->bhcv', w_i, state)
        o_i = jnp.einsum('bhck,bhkv->bhcv', q_i, state) + jnp.einsum('bhcs,bhsv->bhcv', qk_i, v_new)
        state = state * gt_i[..., None, None] + jnp.einsum('bhck,bhcv->bhkv', k_i, v_new)
        return state, o_i

    xs = (jnp.moveaxis(q_dec, 2, 0), jnp.moveaxis(k_dec, 2, 0), jnp.moveaxis(u, 2, 0),
          jnp.moveaxis(w, 2, 0), jnp.moveaxis(qk, 2, 0), jnp.moveaxis(g_tot, 2, 0))
    state0 = jnp.zeros((b, h, dk, dv), jnp.float32)
    _, o = lax.scan(step, state0, xs)
    o = jnp.moveaxis(o, 0, 2)
    return jnp.moveaxis(o, 1, 3).reshape(b, t, h, dv)


def multiscale_causal_pool(p, w_pool, pool_scale):
    b, l, _ = p.shape
    pg = p.astype(jnp.float32).reshape(b, l, POOL_GROUPS, POOL_GROUP_DIM)
    csum = lax.cumsum(pg, axis=1)
    pos = jnp.arange(l)
    outs = []
    for gi, win in enumerate(POOL_WINDOWS):
        c = csum[:, :, gi]
        lagged = jnp.pad(c, ((0, 0), (win, 0), (0, 0)))[:, :l]
        count = jnp.minimum(pos + 1, win).astype(jnp.float32)[None, :, None]
        outs.append((c - lagged) / count - pg[:, :, gi])
    pooled = jnp.stack(outs, axis=2)
    y = jnp.einsum('blgc,gcd->blgd', pooled, w_pool.astype(jnp.float32)).reshape(b, l, D_MODEL)
    return (y * pool_scale.astype(jnp.float32)).astype(p.dtype)


def hybrid_mixer(u, w_in, conv_qkv, a_log, dt_bias, head_norm, w_pool, pool_scale, w_out):
    b, l, _ = u.shape
    proj = jnp.einsum('bld,dp->blp', u, w_in)
    offsets = [int(o) for o in np.cumsum(SPLIT_SIZES)[:-1]]
    qkv, z, b_raw, a_raw, pool_in, gate_pre = jnp.split(proj, offsets, axis=-1)
    qkv = jax.nn.silu(causal_depthwise_conv(qkv, conv_qkv))
    q, k, v = jnp.split(qkv, [DN_KEY_DIM, 2 * DN_KEY_DIM], axis=-1)
    q = l2_normalize(q.reshape(b, l, DN_HEADS, DN_HEAD_DIM))
    k = l2_normalize(k.reshape(b, l, DN_HEADS, DN_HEAD_DIM))
    v = v.reshape(b, l, DN_HEADS, DN_HEAD_DIM).astype(jnp.float32)
    beta = jax.nn.sigmoid(b_raw.astype(jnp.float32))
    g = -jnp.exp(a_log.astype(jnp.float32)) * jax.nn.softplus(
        a_raw.astype(jnp.float32) + dt_bias.astype(jnp.float32))
    pad_r = (-(LEAD_PAD + l)) % CHUNK

    def pad_t(a):
        return jnp.pad(a, [(0, 0), (LEAD_PAD, pad_r)] + [(0, 0)] * (a.ndim - 2))

    o = chunk_gated_delta_rule(pad_t(q), pad_t(k), pad_t(v), pad_t(g), pad_t(beta))
    o = o[:, LEAD_PAD:LEAD_PAD + l]
    zf = z.astype(jnp.float32).reshape(b, l, DN_HEADS, DN_HEAD_DIM)
    o = (o * lax.rsqrt(jnp.mean(o * o, axis=-1, keepdims=True) + NORM_EPS)
         * head_norm.astype(jnp.float32) * jax.nn.silu(zf))
    y_a = o.reshape(b, l, DN_VAL_DIM).astype(u.dtype)
    y_b = multiscale_causal_pool(pool_in, w_pool, pool_scale)
    g_a, g_b = jnp.split(jax.nn.sigmoid(gate_pre), 2, axis=-1)
    y = g_a * y_a + g_b * y_b
    return jnp.einsum('bld,de->ble', y, w_out)


def conv_gated_mlp(u, w_up, conv_ffn, w_down):
    hid = jnp.einsum('bld,df->blf', u, w_up)
    hid = causal_depthwise_conv(hid, conv_ffn)
    gate, val = jnp.split(hid, 2, axis=-1)
    return jnp.einsum('blf,fd->bld', jax.nn.silu(gate) * val, w_down)


def setup_inputs(seed: int = 0) -> dict:
    key = jax.random.key(seed)
    ks = jax.random.split(key, 16)
    f32 = jnp.float32

    def normal(k, shape, scale):
        return jax.random.normal(k, shape, f32) * scale

    x = normal(ks[0], (BATCH, SEQ, D_MODEL), 1.0)
    meta_tokens = normal(ks[1], (N_META, D_MODEL), 1.0)
    norm_mix = 1.0 + normal(ks[2], (DEPTH, D_MODEL), 0.02)
    w_in = normal(ks[3], (DEPTH, D_MODEL, IN_DIM), D_MODEL ** -0.5)
    conv_qkv = normal(ks[4], (DEPTH, DN_CONV, QKV_DIM), DN_CONV ** -0.5)
    a_log = jnp.log(jax.random.uniform(ks[5], (DEPTH, DN_HEADS), f32, 1.0, 16.0))
    dt = jnp.exp(jax.random.uniform(ks[6], (DEPTH, DN_HEADS), f32, math.log(1e-3), math.log(1e-1)))
    dt_bias = dt + jnp.log(-jnp.expm1(-dt))
    head_norm = 1.0 + normal(ks[7], (DEPTH, DN_HEAD_DIM), 0.02)
    w_pool = normal(ks[8], (DEPTH, POOL_GROUPS, POOL_GROUP_DIM, POOL_OUT_GROUP), POOL_GROUP_DIM ** -0.5)
    pool_scale = 1.0 + normal(ks[9], (DEPTH, D_MODEL), 0.02)
    w_out = normal(ks[10], (DEPTH, D_MODEL, D_MODEL), D_MODEL ** -0.5)
    norm_ffn = 1.0 + normal(ks[11], (DEPTH, D_MODEL), 0.02)
    w_up = normal(ks[12], (DEPTH, D_MODEL, 2 * D_FF), D_MODEL ** -0.5)
    conv_ffn = normal(ks[13], (DEPTH, FFN_CONV, 2 * D_FF), FFN_CONV ** -0.5)
    w_down = normal(ks[14], (DEPTH, D_FF, D_MODEL), D_FF ** -0.5)
    norm_final = 1.0 + normal(ks[15], (D_MODEL,), 0.02)
    return {'x': x, 'meta_tokens': meta_tokens, 'norm_mix': norm_mix, 'w_in': w_in,
            'conv_qkv': conv_qkv, 'a_log': a_log, 'dt_bias': dt_bias, 'head_norm': head_norm,
            'w_pool': w_pool, 'pool_scale': pool_scale, 'w_out': w_out, 'norm_ffn': norm_ffn,
            'w_up': w_up, 'conv_ffn': conv_ffn, 'w_down': w_down, 'norm_final': norm_final}


def reference(x, meta_tokens, norm_mix, w_in, conv_qkv, a_log, dt_bias, head_norm, w_pool,
              pool_scale, w_out, norm_ffn, w_up, conv_ffn, w_down, norm_final):
    b = x.shape[0]
    meta = jnp.broadcast_to(meta_tokens.astype(x.dtype)[None], (b, N_META, D_MODEL))
    h = jnp.concatenate([meta, x], axis=1)
    for layer in range(DEPTH):
        h = h + hybrid_mixer(rms_norm(h, norm_mix[layer]), w_in[layer], conv_qkv[layer], a_log[layer],
                             dt_bias[layer], head_norm[layer], w_pool[layer], pool_scale[layer], w_out[layer])
        h = h + conv_gated_mlp(rms_norm(h, norm_ffn[layer]), w_up[layer], conv_ffn[layer], w_down[layer])
    h = rms_norm(h, norm_final)
    return h[:, N_META:]
```

```python
import functools

import jax
import jax.numpy as jnp
from jax import lax
from jax.experimental import pallas as pl
from jax.experimental.pallas import tpu as pltpu

D_MODEL = 1024
N_META = 16
HEADS = 8
HEAD_DIM = 128
QKV_DIM = 3 * D_MODEL
DN_CONV = 4
CHUNK = 64
POOL_WINDOWS = (2, 4, 8, 16)
POOL_GROUP_DIM = 128
POOL_WIDTH = 512
POOL_OUT_GROUP = 256
D_FF = 2816
FFN_CONV = 3
NORM_EPS = 1e-6

TM = 256
LEAD = TM - N_META
LANES = 128
SUBLANES = 8
POOL_CARRY = 16
VMEM_LIMIT = 56 * 1024 * 1024

F32 = jnp.float32
BF16 = jnp.bfloat16


def _sigmoid(x):
    return 1.0 / (1.0 + jnp.exp(-x))


def _silu(x):
    return x * _sigmoid(x)


def _softplus(x):
    return jnp.maximum(x, 0.0) + jnp.log(1.0 + jnp.exp(-jnp.abs(x)))


def _dot(a, b):
    return jnp.dot(a, b, preferred_element_type=F32)


def _dot_nt(a, b):
    return lax.dot_general(a, b, (((1,), (1,)), ((), ())), preferred_element_type=F32)


def _dot_tn(a, b):
    return lax.dot_general(a, b, (((0,), (0,)), ((), ())), preferred_element_type=F32)


def _rms(x, gain):
    ms = jnp.mean(x * x, axis=-1, keepdims=True)
    return x * lax.rsqrt(ms + NORM_EPS) * gain


def _causal_conv(pre, carry_ref, cw, width):
    rows = pre.shape[0]
    ext = jnp.concatenate([carry_ref[...], pre], axis=0)
    carry_ref[...] = pre[rows - SUBLANES:, :]
    acc = pre * cw[width - 1:width, :]
    for j in range(1, width):
        shifted = pltpu.roll(ext, j, axis=0)[SUBLANES:, :]
        acc = acc + shifted * cw[width - 1 - j:width - j, :]
    return acc


def _front_kernel(h_ref, gain_ref, wq_ref, wk_ref, wv_ref, wz_ref, wpb_ref, wga_ref, wgb_ref,
                  cw_ref, avec_ref, dtvec_ref, hn_ref, wpool_ref, ps_ref, tri_ref,
                  q_out, k_out, v_out, a_out, b_out, gcol_out, grow_out,
                  cq, ck, cv, cp):
    i = pl.program_id(1)

    @pl.when(i == 0)
    def _():
        cq[...] = jnp.zeros_like(cq)
        ck[...] = jnp.zeros_like(ck)
        cv[...] = jnp.zeros_like(cv)
        cp[...] = jnp.zeros_like(cp)

    x = h_ref[0]
    xn = _rms(x, gain_ref[...]).astype(BF16)
    cw = cw_ref[...]

    def qkv_section(w_ref, carry_ref, sec):
        pre = _dot(xn, w_ref[...])
        conv = _causal_conv(pre, carry_ref, cw[:, sec * D_MODEL:(sec + 1) * D_MODEL], DN_CONV)
        return _silu(conv)

    def l2norm_store(xs, out_ref, scale):
        for h in range(HEADS):
            hs = slice(h * HEAD_DIM, (h + 1) * HEAD_DIM)
            xh = xs[:, hs]
            ss = jnp.sum(xh * xh, axis=-1, keepdims=True)
            out_ref[0, :, hs] = (xh * (lax.rsqrt(ss + NORM_EPS) * scale)).astype(BF16)

    l2norm_store(qkv_section(wq_ref, cq, 0), q_out, HEAD_DIM ** -0.5)
    l2norm_store(qkv_section(wk_ref, ck, 1), k_out, 1.0)
    v_out[0] = qkv_section(wv_ref, cv, 2).astype(BF16)

    z = _dot(xn, wz_ref[...])
    ga = _sigmoid(_dot(xn, wga_ref[...]))
    a_out[0] = (hn_ref[...] * _silu(z) * ga).astype(BF16)

    pb = _dot(xn, wpb_ref[...])
    p = pb[:, :POOL_WIDTH]
    ba = pb[:, POOL_WIDTH:POOL_WIDTH + LANES]
    row = lax.broadcasted_iota(jnp.int32, (TM, 1), 0)
    pos = i * TM + row - LEAD
    ys = []
    for gi, win in enumerate(POOL_WINDOWS):
        gs = slice(gi * POOL_GROUP_DIM, (gi + 1) * POOL_GROUP_DIM)
        pg = p[:, gs]
        s = jnp.concatenate([cp[:, gs], pg], axis=0)
        sh = 1
        while sh < win:
            s = s + pltpu.roll(s, sh, axis=0)
            sh *= 2
        cnt = jnp.clip(pos + 1, 1, win).astype(F32)
        pooled = s[POOL_CARRY:, :] / cnt - pg
        ys.append(_dot(pooled.astype(BF16), wpool_ref[gi]))
    cp[...] = p[TM - POOL_CARRY:, :]
    yb = jnp.concatenate(ys, axis=-1)
    gb = _sigmoid(_dot(xn, wgb_ref[...]))
    b_out[0] = (gb * yb * ps_ref[...]).astype(BF16)

    lane = lax.broadcasted_iota(jnp.int32, (1, LANES), 1)
    valid = pos >= 0
    beta = jnp.where(valid, _sigmoid(ba), 0.0)
    g = jnp.where(valid, -jnp.exp(avec_ref[...]) * _softplus(ba + dtvec_ref[...]), 0.0)
    g = jnp.where((lane >= HEADS) & (lane < 2 * HEADS), g, 0.0)
    g_hi = g.astype(BF16)
    g_lo = (g - g_hi.astype(F32)).astype(BF16)
    tri = tri_ref[...]
    gc = _dot(tri, g_hi) + _dot(tri, g_lo)
    col = jnp.where(lane < HEADS, beta, gc)
    gcol_out[0] = col[:, :2 * HEADS]
    rowform = col.T
    for c in range(TM // CHUNK):
        grow_out[0, c] = rowform[:2 * HEADS, c * CHUNK:(c + 1) * CHUNK]


def _delta_kernel(q_ref, k_ref, v_ref, gcol_ref, grow_ref, o_ref, state):
    i = pl.program_id(1)

    @pl.when(i == 0)
    def _():
        state[...] = jnp.zeros_like(state)

    ri = lax.broadcasted_iota(jnp.int32, (CHUNK, CHUNK), 0)
    ci = lax.broadcasted_iota(jnp.int32, (CHUNK, CHUNK), 1)
    causal = ri >= ci
    strict = ri > ci
    eye = (ri == ci).astype(F32)

    def chunk_body(c, carry):
        r0 = pl.multiple_of(c * CHUNK, CHUNK)
        gcol = gcol_ref[0, pl.ds(r0, CHUNK), :]
        grow = grow_ref[0, c]
        for h in range(HEADS):
            hs = slice(h * HEAD_DIM, (h + 1) * HEAD_DIM)
            q = q_ref[0, pl.ds(r0, CHUNK), hs]
            k = k_ref[0, pl.ds(r0, CHUNK), hs]
            v = v_ref[0, pl.ds(r0, CHUNK), hs]
            beta = gcol[:, h:h + 1]
            gc_c = gcol[:, HEADS + h:HEADS + h + 1]
            gc_r = grow[HEADS + h:HEADS + h + 1, :]
            g_last = gc_r[:, CHUNK - 1:CHUNK]
            decay = jnp.where(causal, jnp.exp(jnp.minimum(gc_c - gc_r, 0.0)), 0.0)
            kf = k.astype(F32)
            kb = kf * beta
            low = jnp.where(strict, _dot_nt(kb.astype(BF16), k) * decay, 0.0)
            pw = (-low).astype(BF16)
            tinv = eye - low
            for _ in range(5):
                pw_f = _dot(pw, pw)
                pw = pw_f.astype(BF16)
                tinv = tinv + _dot(tinv.astype(BF16), pw)
            tinv_b = tinv.astype(BF16)
            eg = jnp.exp(gc_c)
            u = _dot(tinv_b, (v.astype(F32) * beta).astype(BF16))
            w = _dot(tinv_b, (kb * eg).astype(BF16))
            qk = jnp.where(causal, _dot_nt(q, k) * decay, 0.0)
            q_dec = (q.astype(F32) * eg).astype(BF16)
            k_dec = (kf * jnp.exp(g_last - gc_c)).astype(BF16)
            s = state[h]
            s_b = s.astype(BF16)
            v_new = u - _dot(w.astype(BF16), s_b)
            v_new_b = v_new.astype(BF16)
            o = _dot(q_dec, s_b) + _dot(qk.astype(BF16), v_new_b)
            state[h] = s * jnp.exp(g_last) + _dot_tn(k_dec, v_new_b)
            o_ref[0, pl.ds(r0, CHUNK), hs] = o
        return carry

    lax.fori_loop(0, TM // CHUNK, chunk_body, 0)


def _back_kernel(o_ref, a_ref, b_ref, h_ref, wout_ref, gain_ref, wug_ref, wuv_ref, cwg_ref, cwv_ref,
                 wdown_ref, gfin_ref, out_ref, cg, cv, *, final):
    i = pl.program_id(1)

    @pl.when(i == 0)
    def _():
        cg[...] = jnp.zeros_like(cg)
        cv[...] = jnp.zeros_like(cv)

    ys = []
    for h in range(HEADS):
        hs = slice(h * HEAD_DIM, (h + 1) * HEAD_DIM)
        oh = o_ref[0, :, hs]
        ms = jnp.mean(oh * oh, axis=-1, keepdims=True)
        ys.append(oh * lax.rsqrt(ms + NORM_EPS) * a_ref[0, :, hs].astype(F32)
                  + b_ref[0, :, hs].astype(F32))
    y = jnp.concatenate(ys, axis=-1).astype(BF16)
    h1 = h_ref[0] + _dot(y, wout_ref[...])

    un = _rms(h1, gain_ref[...]).astype(BF16)
    gate = _causal_conv(_dot(un, wug_ref[...]), cg, cwg_ref[...], FFN_CONV)
    val = _causal_conv(_dot(un, wuv_ref[...]), cv, cwv_ref[...], FFN_CONV)
    act = (_silu(gate) * val).astype(BF16)
    h2 = h1 + _dot(act, wdown_ref[...])
    if final:
        h2 = _rms(h2, gfin_ref[...])
    out_ref[0] = h2


def _resident(shape):
    nd = len(shape)
    return pl.BlockSpec(shape, lambda b, i: (0,) * nd, pipeline_mode=pl.Buffered(1))


def _tile(width):
    return pl.BlockSpec((1, TM, width), lambda b, i: (b, i, 0))


def _params():
    return pltpu.CompilerParams(dimension_semantics=("arbitrary", "arbitrary"),
                                vmem_limit_bytes=VMEM_LIMIT)


def _front(h, gain, wq, wk, wv, wz, wpb, wga, wgb, cw, avec, dtvec, hn, wpool, ps, tri):
    bsz, t, _ = h.shape
    nt = t // TM
    nc = TM // CHUNK
    consts = (gain, wq, wk, wv, wz, wpb, wga, wgb, cw, avec, dtvec, hn, wpool, ps, tri)
    act = jax.ShapeDtypeStruct((bsz, t, D_MODEL), BF16)
    return pl.pallas_call(
        _front_kernel,
        grid=(bsz, nt),
        in_specs=[_tile(D_MODEL)] + [_resident(c.shape) for c in consts],
        out_specs=[_tile(D_MODEL)] * 5 + [
            _tile(2 * HEADS),
            pl.BlockSpec((1, nc, 2 * HEADS, CHUNK), lambda b, i: (b, i, 0, 0))],
        out_shape=[act] * 5 + [
            jax.ShapeDtypeStruct((bsz, t, 2 * HEADS), F32),
            jax.ShapeDtypeStruct((bsz, t // CHUNK, 2 * HEADS, CHUNK), F32)],
        scratch_shapes=[pltpu.VMEM((SUBLANES, D_MODEL), F32)] * 3
        + [pltpu.VMEM((POOL_CARRY, POOL_WIDTH), F32)],
        compiler_params=_params(),
        name="front",
    )(h, *consts)


def _delta(q, k, v, gcol, grow):
    bsz, t, _ = q.shape
    nt = t // TM
    nc = TM // CHUNK
    return pl.pallas_call(
        _delta_kernel,
        grid=(bsz, nt),
        in_specs=[_tile(D_MODEL)] * 3 + [
            _tile(2 * HEADS),
            pl.BlockSpec((1, nc, 2 * HEADS, CHUNK), lambda b, i: (b, i, 0, 0))],
        out_specs=_tile(D_MODEL),
        out_shape=jax.ShapeDtypeStruct((bsz, t, D_MODEL), F32),
        scratch_shapes=[pltpu.VMEM((HEADS, HEAD_DIM, HEAD_DIM), F32)],
        compiler_params=_params(),
        name="delta",
    )(q, k, v, gcol, grow)


def _back(o, a, b, h, wout, gain, wug, wuv, cwg, cwv, wdown, gfin, final):
    bsz, t, _ = h.shape
    nt = t // TM
    consts = (wout, gain, wug, wuv, cwg, cwv, wdown, gfin)
    return pl.pallas_call(
        functools.partial(_back_kernel, final=final),
        grid=(bsz, nt),
        in_specs=[_tile(D_MODEL)] * 4 + [_resident(c.shape) for c in consts],
        out_specs=_tile(D_MODEL),
        out_shape=jax.ShapeDtypeStruct((bsz, t, D_MODEL), F32),
        scratch_shapes=[pltpu.VMEM((SUBLANES, D_FF), F32)] * 2,
        compiler_params=_params(),
        name="back",
    )(o, a, b, h, *consts)


def _block_tri():
    r = jnp.arange(TM)
    same_chunk = (r[:, None] // CHUNK) == (r[None, :] // CHUNK)
    return (same_chunk & (r[:, None] >= r[None, :])).astype(BF16)


def _lane_vec(vals, offset):
    return jnp.zeros((1, LANES), F32).at[0, offset:offset + vals.shape[0]].set(vals.astype(F32))


def kernel(x, meta_tokens, norm_mix, w_in, conv_qkv, a_log, dt_bias, head_norm, w_pool,
           pool_scale, w_out, norm_ffn, w_up, conv_ffn, w_down, norm_final):
    bsz, seq, _ = x.shape
    depth = w_in.shape[0]
    meta = jnp.broadcast_to(meta_tokens.astype(x.dtype)[None], (bsz, N_META, D_MODEL))
    h = jnp.concatenate([jnp.zeros((bsz, LEAD, D_MODEL), x.dtype), meta, x], axis=1)
    tri = _block_tri()
    o_qkv, o_z, o_b, o_a, o_pool, o_gate = 0, 3072, 4096, 4104, 4112, 4624
    for layer in range(depth):
        wi = w_in[layer]
        wq = wi[:, 0:D_MODEL].astype(BF16)
        wk = wi[:, D_MODEL:2 * D_MODEL].astype(BF16)
        wv = wi[:, 2 * D_MODEL:3 * D_MODEL].astype(BF16)
        wz = wi[:, o_z:o_b].astype(BF16)
        wpb = jnp.concatenate([wi[:, o_pool:o_gate], wi[:, o_b:o_pool],
                               jnp.zeros((D_MODEL, LANES - 2 * HEADS), wi.dtype)], axis=1).astype(BF16)
        wga = wi[:, o_gate:o_gate + D_MODEL].astype(BF16)
        wgb = wi[:, o_gate + D_MODEL:].astype(BF16)
        q, k, v, a, b, gcol, grow = _front(
            h, norm_mix[layer][None, :], wq, wk, wv, wz, wpb, wga, wgb,
            conv_qkv[layer], _lane_vec(a_log[layer], HEADS), _lane_vec(dt_bias[layer], HEADS),
            jnp.tile(head_norm[layer], HEADS)[None, :], w_pool[layer].astype(BF16),
            pool_scale[layer][None, :], tri)
        o = _delta(q, k, v, gcol, grow)
        wu = w_up[layer]
        cf = conv_ffn[layer]
        h = _back(o, a, b, h, w_out[layer].astype(BF16), norm_ffn[layer][None, :],
                  wu[:, :D_FF].astype(BF16), wu[:, D_FF:].astype(BF16), cf[:, :D_FF], cf[:, D_FF:],
                  w_down[layer].astype(BF16), norm_final[None, :], layer == depth - 1)
    return h[:, TM:]
```

```python
import functools

import jax
import jax.numpy as jnp
from jax import lax
from jax.experimental import pallas as pl
from jax.experimental.pallas import tpu as pltpu

D_MODEL = 1024
N_META = 16
HEADS = 8
HEAD_DIM = 128
QKV_DIM = 3 * D_MODEL
DN_CONV = 4
CHUNK = 64
POOL_WINDOWS = (2, 4, 8, 16)
POOL_GROUP_DIM = 128
POOL_WIDTH = 512
POOL_OUT_GROUP = 256
D_FF = 2816
FFN_CONV = 3
NORM_EPS = 1e-6

TM = 256
LEAD = TM - N_META
LANES = 128
SUBLANES = 8
POOL_CARRY = 16
GROUP = 4
CAT = HEADS * CHUNK
A_CHUNKS = 2
VMEM_LIMIT = 56 * 1024 * 1024

F32 = jnp.float32
BF16 = jnp.bfloat16


def _sigmoid(x):
    return 1.0 / (1.0 + jnp.exp(-x))


def _silu(x):
    return x * _sigmoid(x)


def _softplus(x):
    return jnp.maximum(x, 0.0) + jnp.log(1.0 + jnp.exp(-jnp.abs(x)))


def _dot(a, b):
    return jnp.dot(a, b, preferred_element_type=F32)


def _dot_nt(a, b):
    return lax.dot_general(a, b, (((1,), (1,)), ((), ())), preferred_element_type=F32)


def _dot_tn(a, b):
    return lax.dot_general(a, b, (((0,), (0,)), ((), ())), preferred_element_type=F32)


def _rms(x, gain):
    ms = jnp.mean(x * x, axis=-1, keepdims=True)
    return x * lax.rsqrt(ms + NORM_EPS) * gain


def _causal_conv(pre, carry_ref, cw, width):
    rows = pre.shape[0]
    ext = jnp.concatenate([carry_ref[...], pre], axis=0)
    carry_ref[...] = pre[rows - SUBLANES:, :]
    acc = pre * cw[width - 1:width, :]
    for j in range(1, width):
        shifted = pltpu.roll(ext, j, axis=0)[SUBLANES:, :]
        acc = acc + shifted * cw[width - 1 - j:width - j, :]
    return acc


def _front_kernel(h_ref, gain_ref, wq_ref, wk_ref, wv_ref, wz_ref, wpb_ref, wga_ref, wgb_ref,
                  cw_ref, avec_ref, dtvec_ref, hn_ref, wpool_ref, ps_ref, tri_ref,
                  q_out, k_out, v_out, a_out, b_out, gcol_out, grow_out,
                  cq, ck, cv, cp):
    i = pl.program_id(1)

    @pl.when(i == 0)
    def _():
        cq[...] = jnp.zeros_like(cq)
        ck[...] = jnp.zeros_like(ck)
        cv[...] = jnp.zeros_like(cv)
        cp[...] = jnp.zeros_like(cp)

    x = h_ref[0]
    xn = _rms(x, gain_ref[...]).astype(BF16)
    cw = cw_ref[...]

    def qkv_section(w_ref, carry_ref, sec):
        pre = _dot(xn, w_ref[...])
        conv = _causal_conv(pre, carry_ref, cw[:, sec * D_MODEL:(sec + 1) * D_MODEL], DN_CONV)
        return _silu(conv)

    def l2norm_store(xs, out_ref, scale):
        for h in range(HEADS):
            hs = slice(h * HEAD_DIM, (h + 1) * HEAD_DIM)
            xh = xs[:, hs]
            ss = jnp.sum(xh * xh, axis=-1, keepdims=True)
            out_ref[0, :, hs] = (xh * (lax.rsqrt(ss + NORM_EPS) * scale)).astype(BF16)

    l2norm_store(qkv_section(wq_ref, cq, 0), q_out, HEAD_DIM ** -0.5)
    l2norm_store(qkv_section(wk_ref, ck, 1), k_out, 1.0)
    v_out[0] = qkv_section(wv_ref, cv, 2).astype(BF16)

    z = _dot(xn, wz_ref[...])
    ga = _sigmoid(_dot(xn, wga_ref[...]))
    a_out[0] = (hn_ref[...] * _silu(z) * ga).astype(BF16)

    pb = _dot(xn, wpb_ref[...])
    p = pb[:, :POOL_WIDTH]
    ba = pb[:, POOL_WIDTH:POOL_WIDTH + LANES]
    row = lax.broadcasted_iota(jnp.int32, (TM, 1), 0)
    pos = i * TM + row - LEAD
    ys = []
    for gi, win in enumerate(POOL_WINDOWS):
        gs = slice(gi * POOL_GROUP_DIM, (gi + 1) * POOL_GROUP_DIM)
        pg = p[:, gs]
        s = jnp.concatenate([cp[:, gs], pg], axis=0)
        sh = 1
        while sh < win:
            s = s + pltpu.roll(s, sh, axis=0)
            sh *= 2
        cnt = jnp.clip(pos + 1, 1, win).astype(F32)
        pooled = s[POOL_CARRY:, :] / cnt - pg
        ys.append(_dot(pooled.astype(BF16), wpool_ref[gi]))
    cp[...] = p[TM - POOL_CARRY:, :]
    yb = jnp.concatenate(ys, axis=-1)
    gb = _sigmoid(_dot(xn, wgb_ref[...]))
    b_out[0] = (gb * yb * ps_ref[...]).astype(BF16)

    lane = lax.broadcasted_iota(jnp.int32, (1, LANES), 1)
    valid = pos >= 0
    beta = jnp.where(valid, _sigmoid(ba), 0.0)
    g = jnp.where(valid, -jnp.exp(avec_ref[...]) * _softplus(ba + dtvec_ref[...]), 0.0)
    g = jnp.where((lane >= HEADS) & (lane < 2 * HEADS), g, 0.0)
    g_hi = g.astype(BF16)
    g_lo = (g - g_hi.astype(F32)).astype(BF16)
    tri = tri_ref[...]
    gc = _dot(tri, g_hi) + _dot(tri, g_lo)
    col = jnp.where(lane < HEADS, beta, gc)
    gcol_out[0] = col
    rowform = col.T
    for c in range(TM // CHUNK):
        cs = slice(c * CHUNK, (c + 1) * CHUNK)
        grow_out[0, c] = jnp.concatenate(
            [jnp.concatenate([rowform[r0 + h:r0 + h + 1, cs] for h in range(HEADS)], axis=1)
             for r0 in (0, HEADS)], axis=0)


def _pair_blocks(x):
    a, b = x[:, :HEAD_DIM], x[:, HEAD_DIM:]
    z = jnp.zeros_like(a)
    return jnp.concatenate([jnp.concatenate([a, z], axis=1), jnp.concatenate([z, b], axis=1)], axis=0)


def _delta_kernel(q_ref, k_ref, v_ref, gcol_ref, grow_ref, o_ref,
                  state, tb_s, tbe_s, qk_s, deg_s, kdec_s, gtot_s):
    i = pl.program_id(0)
    nb = q_ref.shape[0]
    gw = GROUP * CHUNK

    @pl.when(i == 0)
    def _():
        state[...] = jnp.zeros_like(state)

    ri = lax.broadcasted_iota(jnp.int32, (CHUNK, gw), 0)
    ci = lax.broadcasted_iota(jnp.int32, (CHUNK, gw), 1) & (CHUNK - 1)
    causal = ri >= ci
    strict = ri > ci
    eye = ri == ci
    bd_mask = ((lax.broadcasted_iota(jnp.int32, (gw, gw), 0) >> 6)
               == (lax.broadcasted_iota(jnp.int32, (gw, gw), 1) >> 6))
    lane_half = lax.broadcasted_iota(jnp.int32, (CHUNK, LANES), 1) >> 6
    zblk = jnp.zeros((CHUNK, HEAD_DIM), BF16)

    def block_diag(pb):
        return jnp.where(bd_mask, jnp.concatenate([pb] * GROUP, axis=0), jnp.zeros((), BF16))

    def lane_bcast(col, idx):
        return jnp.take_along_axis(col, idx, axis=1)

    def phase_a(j, carry):
        xs, ps, meta = [], [], []
        for t in range(A_CHUNKS):
            c = j * A_CHUNKS + t
            rows = pl.ds(pl.multiple_of(c * CHUNK, CHUNK), CHUNK)
            for b in range(nb):
                col = gcol_ref[b, rows, :]
                rowc = grow_ref[b, c]
                beta_r, gc_r = rowc[0:1, :], rowc[1:2, :]
                beta_c = jnp.concatenate(
                    [lane_bcast(col, 2 * p + lane_half) for p in range(HEADS // 2)], axis=1)
                gc_c = jnp.concatenate(
                    [lane_bcast(col, HEADS + 2 * p + lane_half) for p in range(HEADS // 2)], axis=1)
                last = col[CHUNK - 1:CHUNK, :]
                ek_col = jnp.exp(last - col)
                gtot = jnp.exp(last)
                k_all = k_ref[b, rows, :]
                q_all = q_ref[b, rows, :]
                kdec, gts = [], []
                for h in range(HEADS):
                    hs = slice(h * HEAD_DIM, (h + 1) * HEAD_DIM)
                    ek_b = lane_bcast(ek_col, jnp.full((CHUNK, LANES), HEADS + h, jnp.int32))
                    kdec.append((k_all[:, hs].astype(F32) * ek_b).astype(BF16))
                    gts.append(jnp.broadcast_to(gtot[:, HEADS + h:HEADS + h + 1], (SUBLANES, HEAD_DIM)))
                kdec_s[b, rows, :] = jnp.concatenate(kdec, axis=1)
                gtot_s[b, c] = jnp.concatenate(gts, axis=1)
                for g in range(HEADS // GROUP):
                    gl = slice(g * GROUP * HEAD_DIM, (g + 1) * GROUP * HEAD_DIM)
                    cb = slice(g * gw, (g + 1) * gw)
                    kg = k_all[:, gl]
                    lhs = jnp.concatenate([kg, q_all[:, gl]], axis=0)
                    rhs_t = jnp.concatenate(
                        [jnp.concatenate([kg[:, m * HEAD_DIM:(m + 1) * HEAD_DIM] if m == h else zblk
                                          for m in range(GROUP)], axis=1) for h in range(GROUP)], axis=0)
                    sc = _dot_nt(lhs, rhs_t)
                    decay = jnp.exp(jnp.minimum(gc_c[:, cb] - gc_r[:, cb], 0.0))
                    p0 = jnp.where(strict, -(sc[:CHUNK] * beta_c[:, cb] * decay), 0.0)
                    qk_s[b, c, :, cb] = jnp.where(causal, sc[CHUNK:] * decay, 0.0).astype(BF16)
                    eg_r = jnp.exp(gc_r[:, cb])
                    deg_s[b, c, :, cb] = jnp.where(eye, eg_r, 0.0).astype(BF16)
                    xs.append(jnp.where(eye, 1.0, 0.0) + p0)
                    ps.append(p0)
                    meta.append((b, c, cb, beta_r[:, cb], eg_r))
        for s in range(6):
            for u in range(len(xs)):
                pb = ps[u].astype(BF16)
                bd = block_diag(pb)
                if s == 0:
                    ps[u] = _dot(pb, bd)
                elif s < 5:
                    r = _dot(jnp.concatenate([xs[u].astype(BF16), pb], axis=0), bd)
                    xs[u] = xs[u] + r[:CHUNK]
                    ps[u] = r[CHUNK:]
                else:
                    xs[u] = xs[u] + _dot(xs[u].astype(BF16), bd)
        for u, (b, c, cb, beta_r, eg_r) in enumerate(meta):
            tb = xs[u] * beta_r
            tb_s[b, c, :, cb] = tb.astype(BF16)
            tbe_s[b, c, :, cb] = (-(tb * eg_r)).astype(BF16)
        return carry

    lax.fori_loop(0, (TM // CHUNK) // A_CHUNKS, phase_a, 0)

    units = [(b, p) for b in range(nb) for p in range(HEADS // 2)]
    for c in range(TM // CHUNK):
        rows = slice(c * CHUNK, (c + 1) * CHUNK)
        r1s, vns = [], []
        for b, p in units:
            pw = slice(p * 2 * HEAD_DIM, (p + 1) * 2 * HEAD_DIM)
            kq = jnp.concatenate([k_ref[b, rows, pw], q_ref[b, rows, pw]], axis=0)
            r1s.append(_dot(kq, _pair_blocks(state[b, p].astype(BF16))))
        for u, (b, p) in enumerate(units):
            pw = slice(p * 2 * HEAD_DIM, (p + 1) * 2 * HEAD_DIM)
            pc = slice(p * 2 * CHUNK, (p + 1) * 2 * CHUNK)
            ks = r1s[u][:CHUNK].astype(BF16)
            rhs = jnp.concatenate([_pair_blocks(v_ref[b, rows, pw]), _pair_blocks(ks)], axis=0)
            lhs = jnp.concatenate([tb_s[b, c, :, pc], tbe_s[b, c, :, pc]], axis=1)
            vns.append(_pair_blocks(_dot(lhs, rhs).astype(BF16)))
        for u, (b, p) in enumerate(units):
            pw = slice(p * 2 * HEAD_DIM, (p + 1) * 2 * HEAD_DIM)
            kd = kdec_s[b, rows, pw]
            ds = _dot_tn(jnp.concatenate([kd[:, :HEAD_DIM], kd[:, HEAD_DIM:]], axis=0), vns[u])
            state[b, p] = state[b, p] * gtot_s[b, c][0:1, pw] + ds
        for u, (b, p) in enumerate(units):
            pw = slice(p * 2 * HEAD_DIM, (p + 1) * 2 * HEAD_DIM)
            pc = slice(p * 2 * CHUNK, (p + 1) * 2 * CHUNK)
            qs = r1s[u][CHUNK:].astype(BF16)
            rhs = jnp.concatenate([vns[u], _pair_blocks(qs)], axis=0)
            lhs = jnp.concatenate([qk_s[b, c, :, pc], deg_s[b, c, :, pc]], axis=1)
            o_ref[b, rows, pw] = _dot(lhs, rhs)


def _back_kernel(o_ref, a_ref, b_ref, h_ref, wout_ref, gain_ref, wug_ref, wuv_ref, cwg_ref, cwv_ref,
                 wdown_ref, gfin_ref, out_ref, cg, cv, *, final):
    i = pl.program_id(1)

    @pl.when(i == 0)
    def _():
        cg[...] = jnp.zeros_like(cg)
        cv[...] = jnp.zeros_like(cv)

    ys = []
    for h in range(HEADS):
        hs = slice(h * HEAD_DIM, (h + 1) * HEAD_DIM)
        oh = o_ref[0, :, hs]
        ms = jnp.mean(oh * oh, axis=-1, keepdims=True)
        ys.append(oh * lax.rsqrt(ms + NORM_EPS) * a_ref[0, :, hs].astype(F32)
                  + b_ref[0, :, hs].astype(F32))
    y = jnp.concatenate(ys, axis=-1).astype(BF16)
    h1 = h_ref[0] + _dot(y, wout_ref[...])

    un = _rms(h1, gain_ref[...]).astype(BF16)
    gate = _causal_conv(_dot(un, wug_ref[...]), cg, cwg_ref[...], FFN_CONV)
    val = _causal_conv(_dot(un, wuv_ref[...]), cv, cwv_ref[...], FFN_CONV)
    act = (_silu(gate) * val).astype(BF16)
    h2 = h1 + _dot(act, wdown_ref[...])
    if final:
        h2 = _rms(h2, gfin_ref[...])
    out_ref[0] = h2


def _resident(shape):
    nd = len(shape)
    return pl.BlockSpec(shape, lambda b, i: (0,) * nd, pipeline_mode=pl.Buffered(1))


def _tile(width):
    return pl.BlockSpec((1, TM, width), lambda b, i: (b, i, 0))


def _params():
    return pltpu.CompilerParams(dimension_semantics=("arbitrary", "arbitrary"),
                                vmem_limit_bytes=VMEM_LIMIT)


def _front(h, gain, wq, wk, wv, wz, wpb, wga, wgb, cw, avec, dtvec, hn, wpool, ps, tri):
    bsz, t, _ = h.shape
    nt = t // TM
    nc = TM // CHUNK
    consts = (gain, wq, wk, wv, wz, wpb, wga, wgb, cw, avec, dtvec, hn, wpool, ps, tri)
    act = jax.ShapeDtypeStruct((bsz, t, D_MODEL), BF16)
    return pl.pallas_call(
        _front_kernel,
        grid=(bsz, nt),
        in_specs=[_tile(D_MODEL)] + [_resident(c.shape) for c in consts],
        out_specs=[_tile(D_MODEL)] * 5 + [
            _tile(LANES),
            pl.BlockSpec((1, nc, 2, CAT), lambda b, i: (b, i, 0, 0))],
        out_shape=[act] * 5 + [
            jax.ShapeDtypeStruct((bsz, t, LANES), F32),
            jax.ShapeDtypeStruct((bsz, t // CHUNK, 2, CAT), F32)],
        scratch_shapes=[pltpu.VMEM((SUBLANES, D_MODEL), F32)] * 3
        + [pltpu.VMEM((POOL_CARRY, POOL_WIDTH), F32)],
        compiler_params=_params(),
        name="front",
    )(h, *consts)


def _delta(q, k, v, gcol, grow):
    bsz, t, _ = q.shape
    nt = t // TM
    nc = TM // CHUNK
    def both(width):
        return pl.BlockSpec((bsz, TM, width), lambda i: (0, i, 0))

    chunk_mats = pltpu.VMEM((bsz, nc, CHUNK, CAT), BF16)
    return pl.pallas_call(
        _delta_kernel,
        grid=(nt,),
        in_specs=[both(D_MODEL)] * 3 + [
            both(LANES),
            pl.BlockSpec((bsz, nc, 2, CAT), lambda i: (0, i, 0, 0))],
        out_specs=both(D_MODEL),
        out_shape=jax.ShapeDtypeStruct((bsz, t, D_MODEL), F32),
        scratch_shapes=[pltpu.VMEM((bsz, HEADS // 2, HEAD_DIM, 2 * HEAD_DIM), F32)]
        + [chunk_mats] * 4
        + [pltpu.VMEM((bsz, TM, D_MODEL), BF16),
           pltpu.VMEM((bsz, nc, SUBLANES, D_MODEL), F32)],
        compiler_params=pltpu.CompilerParams(dimension_semantics=("arbitrary",),
                                             vmem_limit_bytes=VMEM_LIMIT),
        name="delta",
    )(q, k, v, gcol, grow)


def _back(o, a, b, h, wout, gain, wug, wuv, cwg, cwv, wdown, gfin, final):
    bsz, t, _ = h.shape
    nt = t // TM
    consts = (wout, gain, wug, wuv, cwg, cwv, wdown, gfin)
    return pl.pallas_call(
        functools.partial(_back_kernel, final=final),
        grid=(bsz, nt),
        in_specs=[_tile(D_MODEL)] * 4 + [_resident(c.shape) for c in consts],
        out_specs=_tile(D_MODEL),
        out_shape=jax.ShapeDtypeStruct((bsz, t, D_MODEL), F32),
        scratch_shapes=[pltpu.VMEM((SUBLANES, D_FF), F32)] * 2,
        compiler_params=_params(),
        name="back",
    )(o, a, b, h, *consts)


def _block_tri():
    r = jnp.arange(TM)
    same_chunk = (r[:, None] // CHUNK) == (r[None, :] // CHUNK)
    return (same_chunk & (r[:, None] >= r[None, :])).astype(BF16)


def _lane_vec(vals, offset):
    return jnp.zeros((1, LANES), F32).at[0, offset:offset + vals.shape[0]].set(vals.astype(F32))


def kernel(x, meta_tokens, norm_mix, w_in, conv_qkv, a_log, dt_bias, head_norm, w_pool,
           pool_scale, w_out, norm_ffn, w_up, conv_ffn, w_down, norm_final):
    bsz, seq, _ = x.shape
    depth = w_in.shape[0]
    meta = jnp.broadcast_to(meta_tokens.astype(x.dtype)[None], (bsz, N_META, D_MODEL))
    h = jnp.concatenate([jnp.zeros((bsz, LEAD, D_MODEL), x.dtype), meta, x], axis=1)
    tri = _block_tri()
    o_qkv, o_z, o_b, o_a, o_pool, o_gate = 0, 3072, 4096, 4104, 4112, 4624
    for layer in range(depth):
        wi = w_in[layer]
        wq = wi[:, 0:D_MODEL].astype(BF16)
        wk = wi[:, D_MODEL:2 * D_MODEL].astype(BF16)
        wv = wi[:, 2 * D_MODEL:3 * D_MODEL].astype(BF16)
        wz = wi[:, o_z:o_b].astype(BF16)
        wpb = jnp.concatenate([wi[:, o_pool:o_gate], wi[:, o_b:o_pool],
                               jnp.zeros((D_MODEL, LANES - 2 * HEADS), wi.dtype)], axis=1).astype(BF16)
        wga = wi[:, o_gate:o_gate + D_MODEL].astype(BF16)
        wgb = wi[:, o_gate + D_MODEL:].astype(BF16)
        q, k, v, a, b, gcol, grow = _front(
            h, norm_mix[layer][None, :], wq, wk, wv, wz, wpb, wga, wgb,
            conv_qkv[layer], _lane_vec(a_log[layer], HEADS), _lane_vec(dt_bias[layer], HEADS),
            jnp.tile(head_norm[layer], HEADS)[None, :], w_pool[layer].astype(BF16),
            pool_scale[layer][None, :], tri)
        o = _delta(q, k, v, gcol, grow)
        wu = w_up[layer]
        cf = conv_ffn[layer]
        h = _back(o, a, b, h, w_out[layer].astype(BF16), norm_ffn[layer][None, :],
                  wu[:, :D_FF].astype(BF16), wu[:, D_FF:].astype(BF16), cf[:, :D_FF], cf[:, D_FF:],
                  w_down[layer].astype(BF16), norm_final[None, :], layer == depth - 1)
    return h[:, TM:]
```

```python
import functools

import jax
import jax.numpy as jnp
from jax import lax
from jax.experimental import pallas as pl
from jax.experimental.pallas import tpu as pltpu

D_MODEL = 1024
N_META = 16
HEADS = 8
HEAD_DIM = 128
QKV_DIM = 3 * D_MODEL
DN_CONV = 4
CHUNK = 64
POOL_WINDOWS = (2, 4, 8, 16)
POOL_GROUP_DIM = 128
POOL_WIDTH = 512
POOL_OUT_GROUP = 256
D_FF = 2816
FFN_CONV = 3
NORM_EPS = 1e-6

TM = 256
LEAD = TM - N_META
LANES = 128
SUBLANES = 8
POOL_CARRY = 16
GROUP = 4
CAT = HEADS * CHUNK
A_CHUNKS = 2
MXU_TILE = 256
FF_SPLITS = tuple(range(0, D_FF, 3 * MXU_TILE)) + (D_FF,)
VMEM_LIMIT = 56 * 1024 * 1024

F32 = jnp.float32
BF16 = jnp.bfloat16


def _sigmoid(x):
    return 0.5 * jnp.tanh(0.5 * x) + 0.5


def _silu(x):
    half = 0.5 * x
    return half + half * jnp.tanh(half)


def _softplus(x):
    return jnp.maximum(x, 0.0) + jnp.log(1.0 + jnp.exp(-jnp.abs(x)))


def _dot(a, b):
    return jnp.dot(a, b, preferred_element_type=F32)


def _dot_nt(a, b):
    return lax.dot_general(a, b, (((1,), (1,)), ((), ())), preferred_element_type=F32)


def _dot_tn(a, b):
    return lax.dot_general(a, b, (((0,), (0,)), ((), ())), preferred_element_type=F32)


def _rms(x, gain):
    ms = jnp.mean(x * x, axis=-1, keepdims=True)
    return x * lax.rsqrt(ms + NORM_EPS) * gain


def _causal_conv(buf_ref, cs, cw, width):
    rows = buf_ref.shape[0] - SUBLANES
    groups = rows // SUBLANES
    pre = buf_ref[SUBLANES:, cs]
    cols = pre.shape[1]
    cur = pre.reshape(groups, SUBLANES, cols)
    prev = buf_ref[0:rows, cs].reshape(groups, SUBLANES, cols)
    sub = lax.broadcasted_iota(jnp.int32, (1, SUBLANES, cols), 1)
    acc = pre * cw[width - 1:width, :]
    for j in range(1, width):
        picked = jnp.where(sub < SUBLANES - j, cur, prev)
        shifted = pltpu.roll(picked, j, axis=1).reshape(rows, cols)
        acc = acc + shifted * cw[width - 1 - j:width - j, :]
    buf_ref[0:SUBLANES, cs] = pre[rows - SUBLANES:, :]
    return acc


def _split_stream(refs, from_tokens):
    n = 2 if from_tokens else 1
    return refs[:n], refs[n:]


def _stream_tile(i, h_refs):
    if len(h_refs) == 1:
        return h_refs[0][0]
    tok_ref, first_ref = h_refs
    return jnp.where(i == 0, first_ref[...], tok_ref[0])


def _front_kernel(*refs, from_tokens):
    h_refs, refs = _split_stream(refs, from_tokens)
    (gain_ref, wq_ref, wk_ref, wv_ref, wz_ref, wpb_ref, wga_ref, wgb_ref,
     cw_ref, avec_ref, dtvec_ref, hn_ref, wpool_ref, ps_ref, tri_ref,
     q_out, k_out, v_out, a_out, b_out, gcol_out, grow_out,
     cq, ck, cv, cp, zbuf, gabuf, gbbuf, pbbuf) = refs
    i = pl.program_id(1)

    @pl.when(i == 0)
    def _():
        zeros = jnp.zeros((SUBLANES, D_MODEL), F32)
        cq[0:SUBLANES, :] = zeros
        ck[0:SUBLANES, :] = zeros
        cv[0:SUBLANES, :] = zeros
        cp[...] = jnp.zeros_like(cp)

    x = _stream_tile(i, h_refs)
    xn = _rms(x, gain_ref[...]).astype(BF16)
    cw = cw_ref[...]
    full = slice(0, D_MODEL)

    cq[SUBLANES:, :] = _dot(xn, wq_ref[...])
    ck[SUBLANES:, :] = _dot(xn, wk_ref[...])
    cv[SUBLANES:, :] = _dot(xn, wv_ref[...])
    gabuf[...] = _dot(xn, wga_ref[...])
    zbuf[...] = _dot(xn, wz_ref[...])
    pbbuf[...] = _dot(xn, wpb_ref[...])
    gbbuf[...] = _dot(xn, wgb_ref[...])

    def qkv_section(buf_ref, sec):
        return _silu(_causal_conv(buf_ref, full, cw[:, sec * D_MODEL:(sec + 1) * D_MODEL], DN_CONV))

    def l2norm_store(xs, out_ref, scale):
        for h in range(HEADS):
            hs = slice(h * HEAD_DIM, (h + 1) * HEAD_DIM)
            xh = xs[:, hs]
            ss = jnp.sum(xh * xh, axis=-1, keepdims=True)
            out_ref[0, :, hs] = (xh * (lax.rsqrt(ss + NORM_EPS) * scale)).astype(BF16)

    l2norm_store(qkv_section(cq, 0), q_out, HEAD_DIM ** -0.5)
    l2norm_store(qkv_section(ck, 1), k_out, 1.0)
    v_out[0] = qkv_section(cv, 2).astype(BF16)
    a_out[0] = (hn_ref[...] * _silu(zbuf[...]) * _sigmoid(gabuf[...])).astype(BF16)

    p = pbbuf[:, :POOL_WIDTH]
    ba = pbbuf[:, POOL_WIDTH:POOL_WIDTH + LANES]
    row = lax.broadcasted_iota(jnp.int32, (TM, 1), 0)
    pos = i * TM + row - LEAD
    ys = []
    for gi, win in enumerate(POOL_WINDOWS):
        gs = slice(gi * POOL_GROUP_DIM, (gi + 1) * POOL_GROUP_DIM)
        pg = p[:, gs]
        s = jnp.concatenate([cp[:, gs], pg], axis=0)
        sh = 1
        while sh < win:
            s = s + pltpu.roll(s, sh, axis=0)
            sh *= 2
        cnt = jnp.clip(pos + 1, 1, win).astype(F32)
        pooled = s[POOL_CARRY:, :] / cnt - pg
        ys.append(_dot(pooled.astype(BF16), wpool_ref[gi]))
    cp[...] = p[TM - POOL_CARRY:, :]
    yb = jnp.concatenate(ys, axis=-1)
    b_out[0] = (_sigmoid(gbbuf[...]) * yb * ps_ref[...]).astype(BF16)

    lane = lax.broadcasted_iota(jnp.int32, (1, LANES), 1)
    valid = pos >= 0
    beta = jnp.where(valid, _sigmoid(ba), 0.0)
    g = jnp.where(valid, -jnp.exp(avec_ref[...]) * _softplus(ba + dtvec_ref[...]), 0.0)
    g = jnp.where((lane >= HEADS) & (lane < 2 * HEADS), g, 0.0)
    g_hi = g.astype(BF16)
    g_lo = (g - g_hi.astype(F32)).astype(BF16)
    tri = tri_ref[...]
    gc = _dot(tri, g_hi) + _dot(tri, g_lo)
    col = jnp.where(lane < HEADS, beta, gc)
    gcol_out[0] = col
    rowform = col.T
    for c in range(TM // CHUNK):
        cs = slice(c * CHUNK, (c + 1) * CHUNK)
        grow_out[0, c] = jnp.concatenate(
            [jnp.concatenate([rowform[r0 + h:r0 + h + 1, cs] for h in range(HEADS)], axis=1)
             for r0 in (0, HEADS)], axis=0)


def _pair_blocks(x):
    a, b = x[:, :HEAD_DIM], x[:, HEAD_DIM:]
    z = jnp.zeros_like(a)
    return jnp.concatenate([jnp.concatenate([a, z], axis=1), jnp.concatenate([z, b], axis=1)], axis=0)


def _delta_kernel(q_ref, k_ref, v_ref, gcol_ref, grow_ref, o_ref,
                  state, tb_s, tbe_s, qk_s, deg_s, kdec_s, gtot_s):
    i = pl.program_id(0)
    nb = q_ref.shape[0]
    gw = GROUP * CHUNK

    @pl.when(i == 0)
    def _():
        state[...] = jnp.zeros_like(state)

    ri = lax.broadcasted_iota(jnp.int32, (CHUNK, gw), 0)
    ci = lax.broadcasted_iota(jnp.int32, (CHUNK, gw), 1) & (CHUNK - 1)
    causal = ri >= ci
    strict = ri > ci
    eye = ri == ci
    bd_mask = ((lax.broadcasted_iota(jnp.int32, (gw, gw), 0) >> 6)
               == (lax.broadcasted_iota(jnp.int32, (gw, gw), 1) >> 6))
    lane_half = lax.broadcasted_iota(jnp.int32, (CHUNK, LANES), 1) >> 6
    zblk = jnp.zeros((CHUNK, HEAD_DIM), BF16)

    def block_diag(pb):
        return jnp.where(bd_mask, jnp.concatenate([pb] * GROUP, axis=0), jnp.zeros((), BF16))

    def lane_bcast(col, idx):
        return jnp.take_along_axis(col, idx, axis=1)

    def phase_a(j, carry):
        xs, ps, meta = [], [], []
        for t in range(A_CHUNKS):
            c = j * A_CHUNKS + t
            rows = pl.ds(pl.multiple_of(c * CHUNK, CHUNK), CHUNK)
            for b in range(nb):
                col = gcol_ref[b, rows, :]
                rowc = grow_ref[b, c]
                beta_r, gc_r = rowc[0:1, :], rowc[1:2, :]
                beta_c = jnp.concatenate(
                    [lane_bcast(col, 2 * p + lane_half) for p in range(HEADS // 2)], axis=1)
                gc_c = jnp.concatenate(
                    [lane_bcast(col, HEADS + 2 * p + lane_half) for p in range(HEADS // 2)], axis=1)
                last = col[CHUNK - 1:CHUNK, :]
                ek_col = jnp.exp(last - col)
                gtot = jnp.exp(last)
                k_all = k_ref[b, rows, :]
                q_all = q_ref[b, rows, :]
                kdec, gts = [], []
                for h in range(HEADS):
                    hs = slice(h * HEAD_DIM, (h + 1) * HEAD_DIM)
                    ek_b = lane_bcast(ek_col, jnp.full((CHUNK, LANES), HEADS + h, jnp.int32))
                    kdec.append((k_all[:, hs].astype(F32) * ek_b).astype(BF16))
                    gts.append(jnp.broadcast_to(gtot[:, HEADS + h:HEADS + h + 1], (SUBLANES, HEAD_DIM)))
                kdec_s[b, rows, :] = jnp.concatenate(kdec, axis=1)
                gtot_s[b, c] = jnp.concatenate(gts, axis=1)
                for g in range(HEADS // GROUP):
                    gl = slice(g * GROUP * HEAD_DIM, (g + 1) * GROUP * HEAD_DIM)
                    cb = slice(g * gw, (g + 1) * gw)
                    kg = k_all[:, gl]
                    lhs = jnp.concatenate([kg, q_all[:, gl]], axis=0)
                    rhs_t = jnp.concatenate(
                        [jnp.concatenate([kg[:, m * HEAD_DIM:(m + 1) * HEAD_DIM] if m == h else zblk
                                          for m in range(GROUP)], axis=1) for h in range(GROUP)], axis=0)
                    sc = _dot_nt(lhs, rhs_t)
                    decay = jnp.exp(jnp.minimum(gc_c[:, cb] - gc_r[:, cb], 0.0))
                    p0 = jnp.where(strict, -(sc[:CHUNK] * beta_c[:, cb] * decay), 0.0)
                    qk_s[b, c, :, cb] = jnp.where(causal, sc[CHUNK:] * decay, 0.0).astype(BF16)
                    eg_r = jnp.exp(gc_r[:, cb])
                    deg_s[b, c, :, cb] = jnp.where(eye, eg_r, 0.0).astype(BF16)
                    xs.append(jnp.where(eye, 1.0, 0.0) + p0)
                    ps.append(p0)
                    meta.append((b, c, cb, beta_r[:, cb], eg_r))
        for s in range(6):
            for u in range(len(xs)):
                pb = ps[u].astype(BF16)
                bd = block_diag(pb)
                if s == 0:
                    ps[u] = _dot(pb, bd)
                elif s < 5:
                    r = _dot(jnp.concatenate([xs[u].astype(BF16), pb], axis=0), bd)
                    xs[u] = xs[u] + r[:CHUNK]
                    ps[u] = r[CHUNK:]
                else:
                    xs[u] = xs[u] + _dot(xs[u].astype(BF16), bd)
        for u, (b, c, cb, beta_r, eg_r) in enumerate(meta):
            tb = xs[u] * beta_r
            tb_s[b, c, :, cb] = tb.astype(BF16)
            tbe_s[b, c, :, cb] = (-(tb * eg_r)).astype(BF16)
        return carry

    lax.fori_loop(0, (TM // CHUNK) // A_CHUNKS, phase_a, 0)

    units = [(b, p) for b in range(nb) for p in range(HEADS // 2)]
    for c in range(TM // CHUNK):
        rows = slice(c * CHUNK, (c + 1) * CHUNK)
        r1s, vns = [], []
        for b, p in units:
            pw = slice(p * 2 * HEAD_DIM, (p + 1) * 2 * HEAD_DIM)
            kq = jnp.concatenate([k_ref[b, rows, pw], q_ref[b, rows, pw]], axis=0)
            r1s.append(_dot(kq, _pair_blocks(state[b, p].astype(BF16))))
        for u, (b, p) in enumerate(units):
            pw = slice(p * 2 * HEAD_DIM, (p + 1) * 2 * HEAD_DIM)
            pc = slice(p * 2 * CHUNK, (p + 1) * 2 * CHUNK)
            ks = r1s[u][:CHUNK].astype(BF16)
            rhs = jnp.concatenate([_pair_blocks(v_ref[b, rows, pw]), _pair_blocks(ks)], axis=0)
            lhs = jnp.concatenate([tb_s[b, c, :, pc], tbe_s[b, c, :, pc]], axis=1)
            vns.append(_pair_blocks(_dot(lhs, rhs).astype(BF16)))
        for u, (b, p) in enumerate(units):
            pw = slice(p * 2 * HEAD_DIM, (p + 1) * 2 * HEAD_DIM)
            kd = kdec_s[b, rows, pw]
            ds = _dot_tn(jnp.concatenate([kd[:, :HEAD_DIM], kd[:, HEAD_DIM:]], axis=0), vns[u])
            state[b, p] = state[b, p] * gtot_s[b, c][0:1, pw] + ds
        for u, (b, p) in enumerate(units):
            pw = slice(p * 2 * HEAD_DIM, (p + 1) * 2 * HEAD_DIM)
            pc = slice(p * 2 * CHUNK, (p + 1) * 2 * CHUNK)
            qs = r1s[u][CHUNK:].astype(BF16)
            rhs = jnp.concatenate([vns[u], _pair_blocks(qs)], axis=0)
            lhs = jnp.concatenate([qk_s[b, c, :, pc], deg_s[b, c, :, pc]], axis=1)
            o_ref[b, rows, pw] = _dot(lhs, rhs)


def _back_kernel(o_ref, a_ref, b_ref, *refs, from_tokens, final):
    h_refs, refs = _split_stream(refs, from_tokens)
    (wout_ref, gain_ref, wug_ref, wuv_ref, cwg_ref, cwv_ref, wdown_ref, gfin_ref,
     out_ref, cg, cv) = refs
    i = pl.program_id(1)

    @pl.when(i == 0)
    def _():
        zeros = jnp.zeros((SUBLANES, D_FF), F32)
        cg[0:SUBLANES, :] = zeros
        cv[0:SUBLANES, :] = zeros

    ys = []
    for h in range(HEADS):
        hs = slice(h * HEAD_DIM, (h + 1) * HEAD_DIM)
        oh = o_ref[0, :, hs]
        ms = jnp.mean(oh * oh, axis=-1, keepdims=True)
        ys.append(oh * lax.rsqrt(ms + NORM_EPS) * a_ref[0, :, hs].astype(F32)
                  + b_ref[0, :, hs].astype(F32))
    y = jnp.concatenate(ys, axis=-1).astype(BF16)
    h1 = _stream_tile(i, h_refs) + _dot(y, wout_ref[...])

    un = _rms(h1, gain_ref[...]).astype(BF16)
    cwg, cwv = cwg_ref[...], cwv_ref[...]
    slabs = [slice(lo, hi) for lo, hi in zip(FF_SPLITS[:-1], FF_SPLITS[1:])]

    def up(cs):
        cg[SUBLANES:, cs] = _dot(un, wug_ref[:, cs])
        cv[SUBLANES:, cs] = _dot(un, wuv_ref[:, cs])

    for cs in slabs[:2]:
        up(cs)
    h2 = h1
    for s, cs in enumerate(slabs):
        gate = _causal_conv(cg, cs, cwg[:, cs], FFN_CONV)
        val = _causal_conv(cv, cs, cwv[:, cs], FFN_CONV)
        act = (_silu(gate) * val).astype(BF16)
        if s + 2 < len(slabs):
            up(slabs[s + 2])
        h2 = h2 + _dot(act, wdown_ref[cs, :])
    if final:
        h2 = _rms(h2, gfin_ref[...])
    out_ref[0] = h2


def _resident(shape):
    nd = len(shape)
    return pl.BlockSpec(shape, lambda b, i: (0,) * nd, pipeline_mode=pl.Buffered(1))


def _tile(width):
    return pl.BlockSpec((1, TM, width), lambda b, i: (b, i, 0))


def _params():
    return pltpu.CompilerParams(dimension_semantics=("arbitrary", "arbitrary"),
                                vmem_limit_bytes=VMEM_LIMIT)


def _token_tile():
    return pl.BlockSpec((1, TM, D_MODEL), lambda b, i: (b, jnp.maximum(i - 1, 0), 0))


def _stream_specs(stream):
    if len(stream) == 1:
        return [_tile(D_MODEL)]
    return [_token_tile(), _resident(stream[1].shape)]


def _front(stream, t, gain, wq, wk, wv, wz, wpb, wga, wgb, cw, avec, dtvec, hn, wpool, ps, tri):
    bsz = stream[0].shape[0]
    nt = t // TM
    nc = TM // CHUNK
    consts = (gain, wq, wk, wv, wz, wpb, wga, wgb, cw, avec, dtvec, hn, wpool, ps, tri)
    act = jax.ShapeDtypeStruct((bsz, t, D_MODEL), BF16)
    return pl.pallas_call(
        functools.partial(_front_kernel, from_tokens=len(stream) == 2),
        grid=(bsz, nt),
        in_specs=_stream_specs(stream) + [_resident(c.shape) for c in consts],
        out_specs=[_tile(D_MODEL)] * 5 + [
            _tile(LANES),
            pl.BlockSpec((1, nc, 2, CAT), lambda b, i: (b, i, 0, 0))],
        out_shape=[act] * 5 + [
            jax.ShapeDtypeStruct((bsz, t, LANES), F32),
            jax.ShapeDtypeStruct((bsz, t // CHUNK, 2, CAT), F32)],
        scratch_shapes=[pltpu.VMEM((SUBLANES + TM, D_MODEL), F32)] * 3
        + [pltpu.VMEM((POOL_CARRY, POOL_WIDTH), F32)]
        + [pltpu.VMEM((TM, D_MODEL), F32)] * 3
        + [pltpu.VMEM((TM, POOL_WIDTH + LANES), F32)],
        compiler_params=_params(),
        name="front",
    )(*stream, *consts)


def _delta(q, k, v, gcol, grow):
    bsz, t, _ = q.shape
    nt = t // TM
    nc = TM // CHUNK
    def both(width):
        return pl.BlockSpec((bsz, TM, width), lambda i: (0, i, 0))

    chunk_mats = pltpu.VMEM((bsz, nc, CHUNK, CAT), BF16)
    return pl.pallas_call(
        _delta_kernel,
        grid=(nt,),
        in_specs=[both(D_MODEL)] * 3 + [
            both(LANES),
            pl.BlockSpec((bsz, nc, 2, CAT), lambda i: (0, i, 0, 0))],
        out_specs=both(D_MODEL),
        out_shape=jax.ShapeDtypeStruct((bsz, t, D_MODEL), F32),
        scratch_shapes=[pltpu.VMEM((bsz, HEADS // 2, HEAD_DIM, 2 * HEAD_DIM), F32)]
        + [chunk_mats] * 4
        + [pltpu.VMEM((bsz, TM, D_MODEL), BF16),
           pltpu.VMEM((bsz, nc, SUBLANES, D_MODEL), F32)],
        compiler_params=pltpu.CompilerParams(dimension_semantics=("arbitrary",),
                                             vmem_limit_bytes=VMEM_LIMIT),
        name="delta",
    )(q, k, v, gcol, grow)


def _back(o, a, b, stream, wout, gain, wug, wuv, cwg, cwv, wdown, gfin, final):
    bsz, t, _ = o.shape
    nt = t // TM
    consts = (wout, gain, wug, wuv, cwg, cwv, wdown, gfin)
    if final:
        out_spec, out_len = _token_tile(), t - TM
    else:
        out_spec, out_len = _tile(D_MODEL), t
    return pl.pallas_call(
        functools.partial(_back_kernel, from_tokens=len(stream) == 2, final=final),
        grid=(bsz, nt),
        in_specs=[_tile(D_MODEL)] * 3 + _stream_specs(stream) + [_resident(c.shape) for c in consts],
        out_specs=out_spec,
        out_shape=jax.ShapeDtypeStruct((bsz, out_len, D_MODEL), F32),
        scratch_shapes=[pltpu.VMEM((SUBLANES + TM, D_FF), F32)] * 2,
        compiler_params=_params(),
        name="back",
    )(o, a, b, *stream, *consts)


def _block_tri():
    r = jnp.arange(TM)
    same_chunk = (r[:, None] // CHUNK) == (r[None, :] // CHUNK)
    return (same_chunk & (r[:, None] >= r[None, :])).astype(BF16)


def _lane_vec(vals, offset):
    return jnp.zeros((1, LANES), F32).at[0, offset:offset + vals.shape[0]].set(vals.astype(F32))


def kernel(x, meta_tokens, norm_mix, w_in, conv_qkv, a_log, dt_bias, head_norm, w_pool,
           pool_scale, w_out, norm_ffn, w_up, conv_ffn, w_down, norm_final):
    bsz, seq, _ = x.shape
    depth = w_in.shape[0]
    t = TM + seq
    first = jnp.concatenate([jnp.zeros((LEAD, D_MODEL), x.dtype), meta_tokens.astype(x.dtype)], axis=0)
    stream = (x, first)
    tri = _block_tri()
    o_qkv, o_z, o_b, o_a, o_pool, o_gate = 0, 3072, 4096, 4104, 4112, 4624
    for layer in range(depth):
        wi = w_in[layer]
        wq = wi[:, 0:D_MODEL].astype(BF16)
        wk = wi[:, D_MODEL:2 * D_MODEL].astype(BF16)
        wv = wi[:, 2 * D_MODEL:3 * D_MODEL].astype(BF16)
        wz = wi[:, o_z:o_b].astype(BF16)
        wpb = jnp.concatenate([wi[:, o_pool:o_gate], wi[:, o_b:o_pool],
                               jnp.zeros((D_MODEL, LANES - 2 * HEADS), wi.dtype)], axis=1).astype(BF16)
        wga = wi[:, o_gate:o_gate + D_MODEL].astype(BF16)
        wgb = wi[:, o_gate + D_MODEL:].astype(BF16)
        q, k, v, a, b, gcol, grow = _front(
            stream, t, norm_mix[layer][None, :], wq, wk, wv, wz, wpb, wga, wgb,
            conv_qkv[layer], _lane_vec(a_log[layer], HEADS), _lane_vec(dt_bias[layer], HEADS),
            jnp.tile(head_norm[layer], HEADS)[None, :], w_pool[layer].astype(BF16),
            pool_scale[layer][None, :], tri)
        o = _delta(q, k, v, gcol, grow)
        wu = w_up[layer]
        cf = conv_ffn[layer]
        h = _back(o, a, b, stream, w_out[layer].astype(BF16), norm_ffn[layer][None, :],
                  wu[:, :D_FF].astype(BF16), wu[:, D_FF:].astype(BF16), cf[:, :D_FF], cf[:, D_FF:],
                  w_down[layer].astype(BF16), norm_final[None, :], layer == depth - 1)
        stream = (h,)
    return h
```

```python
import functools

import jax
import jax.numpy as jnp
from jax import lax
from jax.experimental import pallas as pl
from jax.experimental.pallas import tpu as pltpu

D_MODEL = 1024
N_META = 16
HEADS = 8
HEAD_DIM = 128
QKV_DIM = 3 * D_MODEL
DN_CONV = 4
CHUNK = 64
POOL_WINDOWS = (2, 4, 8, 16)
POOL_GROUP_DIM = 128
POOL_WIDTH = 512
POOL_OUT_GROUP = 256
D_FF = 2816
FFN_CONV = 3
NORM_EPS = 1e-6

TM = 512
LEAD = TM - N_META
LANES = 128
SUBLANES = 8
POOL_CARRY = 16
GROUP = 4
CAT = HEADS * CHUNK
A_CHUNKS = 2
MXU_TILE = 256
FF_SPLITS = tuple(range(0, D_FF, 3 * MXU_TILE)) + (D_FF,)
VMEM_LIMIT = 56 * 1024 * 1024

F32 = jnp.float32
BF16 = jnp.bfloat16


def _sigmoid(x):
    return 0.5 * jnp.tanh(0.5 * x) + 0.5


def _silu(x):
    half = 0.5 * x
    return half + half * jnp.tanh(half)


def _softplus(x):
    return jnp.maximum(x, 0.0) + jnp.log(1.0 + jnp.exp(-jnp.abs(x)))


def _dot(a, b):
    return jnp.dot(a, b, preferred_element_type=F32)


def _dot_nt(a, b):
    return lax.dot_general(a, b, (((1,), (1,)), ((), ())), preferred_element_type=F32)


def _dot_tn(a, b):
    return lax.dot_general(a, b, (((0,), (0,)), ((), ())), preferred_element_type=F32)


def _rms(x, gain):
    ms = jnp.mean(x * x, axis=-1, keepdims=True)
    return x * lax.rsqrt(ms + NORM_EPS) * gain


def _causal_conv(buf_ref, cs, cw, width):
    rows = buf_ref.shape[0] - SUBLANES
    groups = rows // SUBLANES
    pre = buf_ref[SUBLANES:, cs]
    cols = pre.shape[1]
    cur = pre.reshape(groups, SUBLANES, cols)
    prev = buf_ref[0:rows, cs].reshape(groups, SUBLANES, cols)
    sub = lax.broadcasted_iota(jnp.int32, (1, SUBLANES, cols), 1)
    acc = pre * cw[width - 1:width, :]
    for j in range(1, width):
        picked = jnp.where(sub < SUBLANES - j, cur, prev)
        shifted = pltpu.roll(picked, j, axis=1).reshape(rows, cols)
        acc = acc + shifted * cw[width - 1 - j:width - j, :]
    buf_ref[0:SUBLANES, cs] = pre[rows - SUBLANES:, :]
    return acc


def _split_stream(refs, from_tokens):
    n = 2 if from_tokens else 1
    return refs[:n], refs[n:]


def _stream_tile(i, h_refs):
    if len(h_refs) == 1:
        return h_refs[0][0]
    tok_ref, first_ref = h_refs
    return jnp.where(i == 0, first_ref[...], tok_ref[0])


def _front_kernel(*refs, from_tokens):
    h_refs, refs = _split_stream(refs, from_tokens)
    (gain_ref, wq_ref, wk_ref, wv_ref, wz_ref, wpb_ref, wga_ref, wgb_ref,
     cw_ref, avec_ref, dtvec_ref, hn_ref, wpool_ref, ps_ref, tri_ref,
     q_out, k_out, v_out, a_out, b_out, gcol_out, grow_out,
     cq, ck, cv, cp, zbuf, gabuf, gbbuf, pbbuf) = refs
    i = pl.program_id(1)

    @pl.when(i == 0)
    def _():
        zeros = jnp.zeros((SUBLANES, D_MODEL), F32)
        cq[0:SUBLANES, :] = zeros
        ck[0:SUBLANES, :] = zeros
        cv[0:SUBLANES, :] = zeros
        cp[...] = jnp.zeros_like(cp)

    x = _stream_tile(i, h_refs)
    xn = _rms(x, gain_ref[...]).astype(BF16)
    cw = cw_ref[...]
    full = slice(0, D_MODEL)

    cq[SUBLANES:, :] = _dot(xn, wq_ref[...])
    ck[SUBLANES:, :] = _dot(xn, wk_ref[...])
    cv[SUBLANES:, :] = _dot(xn, wv_ref[...])
    gabuf[...] = _dot(xn, wga_ref[...])
    zbuf[...] = _dot(xn, wz_ref[...])
    pbbuf[...] = _dot(xn, wpb_ref[...])
    gbbuf[...] = _dot(xn, wgb_ref[...])

    def qkv_section(buf_ref, sec):
        return _silu(_causal_conv(buf_ref, full, cw[:, sec * D_MODEL:(sec + 1) * D_MODEL], DN_CONV))

    def l2norm_store(xs, out_ref, scale):
        for h in range(HEADS):
            hs = slice(h * HEAD_DIM, (h + 1) * HEAD_DIM)
            xh = xs[:, hs]
            ss = jnp.sum(xh * xh, axis=-1, keepdims=True)
            out_ref[0, :, hs] = (xh * (lax.rsqrt(ss + NORM_EPS) * scale)).astype(BF16)

    l2norm_store(qkv_section(cq, 0), q_out, HEAD_DIM ** -0.5)
    l2norm_store(qkv_section(ck, 1), k_out, 1.0)
    v_out[0] = qkv_section(cv, 2).astype(BF16)
    a_out[0] = (hn_ref[...] * _silu(zbuf[...]) * _sigmoid(gabuf[...])).astype(BF16)

    p = pbbuf[:, :POOL_WIDTH]
    ba = pbbuf[:, POOL_WIDTH:POOL_WIDTH + LANES]
    row = lax.broadcasted_iota(jnp.int32, (TM, 1), 0)
    pos = i * TM + row - LEAD
    ys = []
    for gi, win in enumerate(POOL_WINDOWS):
        gs = slice(gi * POOL_GROUP_DIM, (gi + 1) * POOL_GROUP_DIM)
        pg = p[:, gs]
        s = jnp.concatenate([cp[:, gs], pg], axis=0)
        sh = 1
        while sh < win:
            s = s + pltpu.roll(s, sh, axis=0)
            sh *= 2
        cnt = jnp.clip(pos + 1, 1, win).astype(F32)
        pooled = s[POOL_CARRY:, :] / cnt - pg
        ys.append(_dot(pooled.astype(BF16), wpool_ref[gi]))
    cp[...] = p[TM - POOL_CARRY:, :]
    yb = jnp.concatenate(ys, axis=-1)
    b_out[0] = (_sigmoid(gbbuf[...]) * yb * ps_ref[...]).astype(BF16)

    lane = lax.broadcasted_iota(jnp.int32, (1, LANES), 1)
    valid = pos >= 0
    beta = jnp.where(valid, _sigmoid(ba), 0.0)
    g = jnp.where(valid, -jnp.exp(avec_ref[...]) * _softplus(ba + dtvec_ref[...]), 0.0)
    g = jnp.where((lane >= HEADS) & (lane < 2 * HEADS), g, 0.0)
    g_hi = g.astype(BF16)
    g_lo = (g - g_hi.astype(F32)).astype(BF16)
    tri = tri_ref[...]
    gc = _dot(tri, g_hi) + _dot(tri, g_lo)
    col = jnp.where(lane < HEADS, beta, gc)
    gcol_out[0] = col
    rowform = col.T
    for c in range(TM // CHUNK):
        cs = slice(c * CHUNK, (c + 1) * CHUNK)
        grow_out[0, c] = jnp.concatenate(
            [jnp.concatenate([rowform[r0 + h:r0 + h + 1, cs] for h in range(HEADS)], axis=1)
             for r0 in (0, HEADS)], axis=0)


def _pair_blocks(x):
    a, b = x[:, :HEAD_DIM], x[:, HEAD_DIM:]
    z = jnp.zeros_like(a)
    return jnp.concatenate([jnp.concatenate([a, z], axis=1), jnp.concatenate([z, b], axis=1)], axis=0)


def _delta_kernel(q_ref, k_ref, v_ref, gcol_ref, grow_ref, o_ref,
                  state, tb_s, tbe_s, qk_s, deg_s, kdec_s, gtot_s):
    i = pl.program_id(0)
    nb = q_ref.shape[0]
    gw = GROUP * CHUNK

    @pl.when(i == 0)
    def _():
        state[...] = jnp.zeros_like(state)

    ri = lax.broadcasted_iota(jnp.int32, (CHUNK, gw), 0)
    ci = lax.broadcasted_iota(jnp.int32, (CHUNK, gw), 1) & (CHUNK - 1)
    causal = ri >= ci
    strict = ri > ci
    eye = ri == ci
    bd_mask = ((lax.broadcasted_iota(jnp.int32, (gw, gw), 0) >> 6)
               == (lax.broadcasted_iota(jnp.int32, (gw, gw), 1) >> 6))
    lane_half = lax.broadcasted_iota(jnp.int32, (CHUNK, LANES), 1) >> 6
    zblk = jnp.zeros((CHUNK, HEAD_DIM), BF16)

    def block_diag(pb):
        return jnp.where(bd_mask, jnp.concatenate([pb] * GROUP, axis=0), jnp.zeros((), BF16))

    def lane_bcast(col, idx):
        return jnp.take_along_axis(col, idx, axis=1)

    def phase_a(j, carry):
        xs, ps, meta = [], [], []
        for t in range(A_CHUNKS):
            c = j * A_CHUNKS + t
            rows = pl.ds(pl.multiple_of(c * CHUNK, CHUNK), CHUNK)
            for b in range(nb):
                col = gcol_ref[b, rows, :]
                rowc = grow_ref[b, c]
                beta_r, gc_r = rowc[0:1, :], rowc[1:2, :]
                beta_c = jnp.concatenate(
                    [lane_bcast(col, 2 * p + lane_half) for p in range(HEADS // 2)], axis=1)
                gc_c = jnp.concatenate(
                    [lane_bcast(col, HEADS + 2 * p + lane_half) for p in range(HEADS // 2)], axis=1)
                last = col[CHUNK - 1:CHUNK, :]
                ek_col = jnp.exp(last - col)
                gtot = jnp.exp(last)
                k_all = k_ref[b, rows, :]
                q_all = q_ref[b, rows, :]
                kdec, gts = [], []
                for h in range(HEADS):
                    hs = slice(h * HEAD_DIM, (h + 1) * HEAD_DIM)
                    ek_b = lane_bcast(ek_col, jnp.full((CHUNK, LANES), HEADS + h, jnp.int32))
                    kdec.append((k_all[:, hs].astype(F32) * ek_b).astype(BF16))
                    gts.append(jnp.broadcast_to(gtot[:, HEADS + h:HEADS + h + 1], (SUBLANES, HEAD_DIM)))
                kdec_s[b, rows, :] = jnp.concatenate(kdec, axis=1)
                gtot_s[b, c] = jnp.concatenate(gts, axis=1)
                for g in range(HEADS // GROUP):
                    gl = slice(g * GROUP * HEAD_DIM, (g + 1) * GROUP * HEAD_DIM)
                    cb = slice(g * gw, (g + 1) * gw)
                    kg = k_all[:, gl]
                    lhs = jnp.concatenate([kg, q_all[:, gl]], axis=0)
                    rhs_t = jnp.concatenate(
                        [jnp.concatenate([kg[:, m * HEAD_DIM:(m + 1) * HEAD_DIM] if m == h else zblk
                                          for m in range(GROUP)], axis=1) for h in range(GROUP)], axis=0)
                    sc = _dot_nt(lhs, rhs_t)
                    decay = jnp.exp(jnp.minimum(gc_c[:, cb] - gc_r[:, cb], 0.0))
                    p0 = jnp.where(strict, -(sc[:CHUNK] * beta_c[:, cb] * decay), 0.0)
                    qk_s[b, c, :, cb] = jnp.where(causal, sc[CHUNK:] * decay, 0.0).astype(BF16)
                    eg_r = jnp.exp(gc_r[:, cb])
                    deg_s[b, c, :, cb] = jnp.where(eye, eg_r, 0.0).astype(BF16)
                    xs.append(jnp.where(eye, 1.0, 0.0) + p0)
                    ps.append(p0)
                    meta.append((b, c, cb, beta_r[:, cb], eg_r))
        for s in range(6):
            for u in range(len(xs)):
                pb = ps[u].astype(BF16)
                bd = block_diag(pb)
                if s == 0:
                    ps[u] = _dot(pb, bd)
                elif s < 5:
                    r = _dot(jnp.concatenate([xs[u].astype(BF16), pb], axis=0), bd)
                    xs[u] = xs[u] + r[:CHUNK]
                    ps[u] = r[CHUNK:]
                else:
                    xs[u] = xs[u] + _dot(xs[u].astype(BF16), bd)
        for u, (b, c, cb, beta_r, eg_r) in enumerate(meta):
            tb = xs[u] * beta_r
            tb_s[b, c, :, cb] = tb.astype(BF16)
            tbe_s[b, c, :, cb] = (-(tb * eg_r)).astype(BF16)
        return carry

    lax.fori_loop(0, (TM // CHUNK) // A_CHUNKS, phase_a, 0)

    units = [(b, p) for b in range(nb) for p in range(HEADS // 2)]
    for c in range(TM // CHUNK):
        rows = slice(c * CHUNK, (c + 1) * CHUNK)
        r1s, vns = [], []
        for b, p in units:
            pw = slice(p * 2 * HEAD_DIM, (p + 1) * 2 * HEAD_DIM)
            kq = jnp.concatenate([k_ref[b, rows, pw], q_ref[b, rows, pw]], axis=0)
            r1s.append(_dot(kq, _pair_blocks(state[b, p].astype(BF16))))
        for u, (b, p) in enumerate(units):
            pw = slice(p * 2 * HEAD_DIM, (p + 1) * 2 * HEAD_DIM)
            pc = slice(p * 2 * CHUNK, (p + 1) * 2 * CHUNK)
            ks = r1s[u][:CHUNK].astype(BF16)
            rhs = jnp.concatenate([_pair_blocks(v_ref[b, rows, pw]), _pair_blocks(ks)], axis=0)
            lhs = jnp.concatenate([tb_s[b, c, :, pc], tbe_s[b, c, :, pc]], axis=1)
            vns.append(_pair_blocks(_dot(lhs, rhs).astype(BF16)))
        for u, (b, p) in enumerate(units):
            pw = slice(p * 2 * HEAD_DIM, (p + 1) * 2 * HEAD_DIM)
            kd = kdec_s[b, rows, pw]
            ds = _dot_tn(jnp.concatenate([kd[:, :HEAD_DIM], kd[:, HEAD_DIM:]], axis=0), vns[u])
            state[b, p] = state[b, p] * gtot_s[b, c][0:1, pw] + ds
        for u, (b, p) in enumerate(units):
            pw = slice(p * 2 * HEAD_DIM, (p + 1) * 2 * HEAD_DIM)
            pc = slice(p * 2 * CHUNK, (p + 1) * 2 * CHUNK)
            qs = r1s[u][CHUNK:].astype(BF16)
            rhs = jnp.concatenate([vns[u], _pair_blocks(qs)], axis=0)
            lhs = jnp.concatenate([qk_s[b, c, :, pc], deg_s[b, c, :, pc]], axis=1)
            o_ref[b, rows, pw] = _dot(lhs, rhs)


def _back_kernel(o_ref, a_ref, b_ref, *refs, from_tokens, final):
    h_refs, refs = _split_stream(refs, from_tokens)
    (wout_ref, gain_ref, wug_ref, wuv_ref, cwg_ref, cwv_ref, wdown_ref, gfin_ref,
     out_ref, cg, cv) = refs
    i = pl.program_id(1)

    @pl.when(i == 0)
    def _():
        zeros = jnp.zeros((SUBLANES, D_FF), F32)
        cg[0:SUBLANES, :] = zeros
        cv[0:SUBLANES, :] = zeros

    ys = []
    for h in range(HEADS):
        hs = slice(h * HEAD_DIM, (h + 1) * HEAD_DIM)
        oh = o_ref[0, :, hs]
        ms = jnp.mean(oh * oh, axis=-1, keepdims=True)
        ys.append(oh * lax.rsqrt(ms + NORM_EPS) * a_ref[0, :, hs].astype(F32)
                  + b_ref[0, :, hs].astype(F32))
    y = jnp.concatenate(ys, axis=-1).astype(BF16)
    h1 = _stream_tile(i, h_refs) + _dot(y, wout_ref[...])

    un = _rms(h1, gain_ref[...]).astype(BF16)
    cwg, cwv = cwg_ref[...], cwv_ref[...]
    slabs = [slice(lo, hi) for lo, hi in zip(FF_SPLITS[:-1], FF_SPLITS[1:])]

    def up(cs):
        cg[SUBLANES:, cs] = _dot(un, wug_ref[:, cs])
        cv[SUBLANES:, cs] = _dot(un, wuv_ref[:, cs])

    for cs in slabs[:2]:
        up(cs)
    h2 = h1
    for s, cs in enumerate(slabs):
        gate = _causal_conv(cg, cs, cwg[:, cs], FFN_CONV)
        val = _causal_conv(cv, cs, cwv[:, cs], FFN_CONV)
        act = (_silu(gate) * val).astype(BF16)
        if s + 2 < len(slabs):
            up(slabs[s + 2])
        h2 = h2 + _dot(act, wdown_ref[cs, :])
    if final:
        h2 = _rms(h2, gfin_ref[...])
    out_ref[0] = h2


def _resident(shape):
    nd = len(shape)
    return pl.BlockSpec(shape, lambda b, i: (0,) * nd, pipeline_mode=pl.Buffered(1))


def _tile(width):
    return pl.BlockSpec((1, TM, width), lambda b, i: (b, i, 0))


def _params():
    return pltpu.CompilerParams(dimension_semantics=("arbitrary", "arbitrary"),
                                vmem_limit_bytes=VMEM_LIMIT)


def _token_tile():
    return pl.BlockSpec((1, TM, D_MODEL), lambda b, i: (b, jnp.maximum(i - 1, 0), 0))


def _stream_specs(stream):
    if len(stream) == 1:
        return [_tile(D_MODEL)]
    return [_token_tile(), _resident(stream[1].shape)]


def _front(stream, t, gain, wq, wk, wv, wz, wpb, wga, wgb, cw, avec, dtvec, hn, wpool, ps, tri):
    bsz = stream[0].shape[0]
    nt = t // TM
    nc = TM // CHUNK
    consts = (gain, wq, wk, wv, wz, wpb, wga, wgb, cw, avec, dtvec, hn, wpool, ps, tri)
    act = jax.ShapeDtypeStruct((bsz, t, D_MODEL), BF16)
    return pl.pallas_call(
        functools.partial(_front_kernel, from_tokens=len(stream) == 2),
        grid=(bsz, nt),
        in_specs=_stream_specs(stream) + [_resident(c.shape) for c in consts],
        out_specs=[_tile(D_MODEL)] * 5 + [
            _tile(LANES),
            pl.BlockSpec((1, nc, 2, CAT), lambda b, i: (b, i, 0, 0))],
        out_shape=[act] * 5 + [
            jax.ShapeDtypeStruct((bsz, t, LANES), F32),
            jax.ShapeDtypeStruct((bsz, t // CHUNK, 2, CAT), F32)],
        scratch_shapes=[pltpu.VMEM((SUBLANES + TM, D_MODEL), F32)] * 3
        + [pltpu.VMEM((POOL_CARRY, POOL_WIDTH), F32)]
        + [pltpu.VMEM((TM, D_MODEL), F32)] * 3
        + [pltpu.VMEM((TM, POOL_WIDTH + LANES), F32)],
        compiler_params=_params(),
        name="front",
    )(*stream, *consts)


def _delta(q, k, v, gcol, grow):
    bsz, t, _ = q.shape
    nt = t // TM
    nc = TM // CHUNK
    def both(width):
        return pl.BlockSpec((bsz, TM, width), lambda i: (0, i, 0))

    chunk_mats = pltpu.VMEM((bsz, nc, CHUNK, CAT), BF16)
    return pl.pallas_call(
        _delta_kernel,
        grid=(nt,),
        in_specs=[both(D_MODEL)] * 3 + [
            both(LANES),
            pl.BlockSpec((bsz, nc, 2, CAT), lambda i: (0, i, 0, 0))],
        out_specs=both(D_MODEL),
        out_shape=jax.ShapeDtypeStruct((bsz, t, D_MODEL), F32),
        scratch_shapes=[pltpu.VMEM((bsz, HEADS // 2, HEAD_DIM, 2 * HEAD_DIM), F32)]
        + [chunk_mats] * 4
        + [pltpu.VMEM((bsz, TM, D_MODEL), BF16),
           pltpu.VMEM((bsz, nc, SUBLANES, D_MODEL), F32)],
        compiler_params=pltpu.CompilerParams(dimension_semantics=("arbitrary",),
                                             vmem_limit_bytes=VMEM_LIMIT),
        name="delta",
    )(q, k, v, gcol, grow)


def _back(o, a, b, stream, wout, gain, wug, wuv, cwg, cwv, wdown, gfin, final):
    bsz, t, _ = o.shape
    nt = t // TM
    consts = (wout, gain, wug, wuv, cwg, cwv, wdown, gfin)
    if final:
        out_spec, out_len = _token_tile(), t - TM
    else:
        out_spec, out_len = _tile(D_MODEL), t
    return pl.pallas_call(
        functools.partial(_back_kernel, from_tokens=len(stream) == 2, final=final),
        grid=(bsz, nt),
        in_specs=[_tile(D_MODEL)] * 3 + _stream_specs(stream) + [_resident(c.shape) for c in consts],
        out_specs=out_spec,
        out_shape=jax.ShapeDtypeStruct((bsz, out_len, D_MODEL), F32),
        scratch_shapes=[pltpu.VMEM((SUBLANES + TM, D_FF), F32)] * 2,
        compiler_params=_params(),
        name="back",
    )(o, a, b, *stream, *consts)


def _block_tri():
    r = jnp.arange(TM)
    same_chunk = (r[:, None] // CHUNK) == (r[None, :] // CHUNK)
    return (same_chunk & (r[:, None] >= r[None, :])).astype(BF16)


def _lane_vec(vals, offset):
    return jnp.zeros((1, LANES), F32).at[0, offset:offset + vals.shape[0]].set(vals.astype(F32))


def kernel(x, meta_tokens, norm_mix, w_in, conv_qkv, a_log, dt_bias, head_norm, w_pool,
           pool_scale, w_out, norm_ffn, w_up, conv_ffn, w_down, norm_final):
    bsz, seq, _ = x.shape
    depth = w_in.shape[0]
    t = TM + seq
    first = jnp.concatenate([jnp.zeros((LEAD, D_MODEL), x.dtype), meta_tokens.astype(x.dtype)], axis=0)
    stream = (x, first)
    tri = _block_tri()
    o_qkv, o_z, o_b, o_a, o_pool, o_gate = 0, 3072, 4096, 4104, 4112, 4624
    for layer in range(depth):
        wi = w_in[layer]
        wq = wi[:, 0:D_MODEL].astype(BF16)
        wk = wi[:, D_MODEL:2 * D_MODEL].astype(BF16)
        wv = wi[:, 2 * D_MODEL:3 * D_MODEL].astype(BF16)
        wz = wi[:, o_z:o_b].astype(BF16)
        wpb = jnp.concatenate([wi[:, o_pool:o_gate], wi[:, o_b:o_pool],
                               jnp.zeros((D_MODEL, LANES - 2 * HEADS), wi.dtype)], axis=1).astype(BF16)
        wga = wi[:, o_gate:o_gate + D_MODEL].astype(BF16)
        wgb = wi[:, o_gate + D_MODEL:].astype(BF16)
        q, k, v, a, b, gcol, grow = _front(
            stream, t, norm_mix[layer][None, :], wq, wk, wv, wz, wpb, wga, wgb,
            conv_qkv[layer], _lane_vec(a_log[layer], HEADS), _lane_vec(dt_bias[layer], HEADS),
            jnp.tile(head_norm[layer], HEADS)[None, :], w_pool[layer].astype(BF16),
            pool_scale[layer][None, :], tri)
        o = _delta(q, k, v, gcol, grow)
        wu = w_up[layer]
        cf = conv_ffn[layer]
        h = _back(o, a, b, stream, w_out[layer].astype(BF16), norm_ffn[layer][None, :],
                  wu[:, :D_FF].astype(BF16), wu[:, D_FF:].astype(BF16), cf[:, :D_FF], cf[:, D_FF:],
                  w_down[layer].astype(BF16), norm_final[None, :], layer == depth - 1)
        stream = (h,)
    return h
```

```python
import functools

import jax
import jax.numpy as jnp
from jax import lax
from jax.experimental import pallas as pl
from jax.experimental.pallas import tpu as pltpu

D_MODEL = 1024
N_META = 16
HEADS = 8
HEAD_DIM = 128
QKV_DIM = 3 * D_MODEL
DN_CONV = 4
CHUNK = 64
POOL_WINDOWS = (2, 4, 8, 16)
POOL_GROUP_DIM = 128
POOL_WIDTH = 512
POOL_OUT_GROUP = 256
D_FF = 2816
FFN_CONV = 3
NORM_EPS = 1e-6

TM = 512
LEAD = TM - N_META
LANES = 128
SUBLANES = 8
POOL_CARRY = 16
GROUP = 4
CAT = HEADS * CHUNK
A_CHUNKS = 2
MXU_TILE = 256
FF_SPLITS = tuple(range(0, D_FF, 3 * MXU_TILE)) + (D_FF,)
VMEM_LIMIT = 56 * 1024 * 1024

_W_IN_SECTIONS = (("q", D_MODEL), ("k", D_MODEL), ("v", D_MODEL), ("z", D_MODEL),
                  ("gate_a", D_MODEL), ("gate_b", D_MODEL), ("pool_ba", POOL_WIDTH + LANES))
W_IN_COLS = {}
W_IN_WIDTH = 0
for _name, _width in _W_IN_SECTIONS:
    W_IN_COLS[_name] = (W_IN_WIDTH, W_IN_WIDTH + _width)
    W_IN_WIDTH += _width

F32 = jnp.float32
BF16 = jnp.bfloat16


def _sigmoid(x):
    return 0.5 * jnp.tanh(0.5 * x) + 0.5


def _silu_of_twice(half):
    return half + half * jnp.tanh(half)


def _softplus(x):
    return jnp.maximum(x, 0.0) + jnp.log(1.0 + jnp.exp(-jnp.abs(x)))


def _dot(a, b):
    return jnp.dot(a, b, preferred_element_type=F32)


def _dot_nt(a, b):
    return lax.dot_general(a, b, (((1,), (1,)), ((), ())), preferred_element_type=F32)


def _dot_tn(a, b):
    return lax.dot_general(a, b, (((0,), (0,)), ((), ())), preferred_element_type=F32)


def _rms(x, gain):
    ms = jnp.mean(x * x, axis=-1, keepdims=True)
    return x * lax.rsqrt(ms + NORM_EPS) * gain


def _causal_conv(buf_ref, cs, cw, width):
    rows = buf_ref.shape[0] - SUBLANES
    groups = rows // SUBLANES
    pre = buf_ref[SUBLANES:, cs]
    cols = pre.shape[1]
    cur = pre.reshape(groups, SUBLANES, cols)
    prev = buf_ref[0:rows, cs].reshape(groups, SUBLANES, cols)
    sub = lax.broadcasted_iota(jnp.int32, (1, SUBLANES, cols), 1)
    acc = pre * cw[width - 1:width, :]
    for j in range(1, width):
        picked = jnp.where(sub < SUBLANES - j, cur, prev)
        shifted = pltpu.roll(picked, j, axis=1).reshape(rows, cols)
        acc = acc + shifted * cw[width - 1 - j:width - j, :]
    buf_ref[0:SUBLANES, cs] = pre[rows - SUBLANES:, :]
    return acc


def _split_stream(refs, from_tokens):
    n = 2 if from_tokens else 1
    return refs[:n], refs[n:]


def _stream_tile(i, h_refs):
    if len(h_refs) == 1:
        return h_refs[0][0]
    tok_ref, first_ref = h_refs
    return jnp.where(i == 0, first_ref[...], tok_ref[0])


def _front_kernel(*refs, from_tokens):
    h_refs, refs = _split_stream(refs, from_tokens)
    (gain_ref, w_ref, cw_ref, avec_ref, dtvec_ref, hn_ref, wpool_ref, ps_ref, tri_ref,
     q_out, k_out, v_out, a_out, b_out, gcol_out, grow_out,
     cq, ck, cv, cp, zbuf, gabuf, gbbuf, pbbuf) = refs
    i = pl.program_id(1)

    @pl.when(i == 0)
    def _():
        zeros = jnp.zeros((SUBLANES, D_MODEL), F32)
        cq[0:SUBLANES, :] = zeros
        ck[0:SUBLANES, :] = zeros
        cv[0:SUBLANES, :] = zeros
        cp[...] = jnp.zeros_like(cp)

    x = _stream_tile(i, h_refs)
    xn = _rms(x, gain_ref[...]).astype(BF16)
    cw = cw_ref[...]
    full = slice(0, D_MODEL)

    def proj(name):
        lo, hi = W_IN_COLS[name]
        return _dot(xn, w_ref[:, lo:hi])

    cq[SUBLANES:, :] = proj("q")
    ck[SUBLANES:, :] = proj("k")
    cv[SUBLANES:, :] = proj("v")
    gabuf[...] = proj("gate_a")
    zbuf[...] = proj("z")
    pbbuf[...] = proj("pool_ba")
    gbbuf[...] = proj("gate_b")

    def qkv_section(buf_ref, sec):
        return _silu_of_twice(
            _causal_conv(buf_ref, full, cw[:, sec * D_MODEL:(sec + 1) * D_MODEL], DN_CONV))

    def l2norm_store(xs, out_ref, scale):
        for h in range(HEADS):
            hs = slice(h * HEAD_DIM, (h + 1) * HEAD_DIM)
            xh = xs[:, hs]
            ss = jnp.sum(xh * xh, axis=-1, keepdims=True)
            out_ref[0, :, hs] = (xh * (lax.rsqrt(ss + NORM_EPS) * scale)).astype(BF16)

    l2norm_store(qkv_section(cq, 0), q_out, HEAD_DIM ** -0.5)
    l2norm_store(qkv_section(ck, 1), k_out, 1.0)
    v_out[0] = qkv_section(cv, 2).astype(BF16)
    a_out[0] = (hn_ref[...] * _silu_of_twice(zbuf[...]) * (1.0 + jnp.tanh(gabuf[...]))).astype(BF16)

    p = pbbuf[:, :POOL_WIDTH]
    ba = pbbuf[:, POOL_WIDTH:POOL_WIDTH + LANES]
    row = lax.broadcasted_iota(jnp.int32, (TM, 1), 0)
    pos = i * TM + row - LEAD
    ys = []
    for gi, win in enumerate(POOL_WINDOWS):
        gs = slice(gi * POOL_GROUP_DIM, (gi + 1) * POOL_GROUP_DIM)
        pg = p[:, gs]
        s = jnp.concatenate([cp[:, gs], pg], axis=0)
        sh = 1
        while sh < win:
            s = s + pltpu.roll(s, sh, axis=0)
            sh *= 2
        cnt = jnp.clip(pos + 1, 1, win).astype(F32)
        pooled = s[POOL_CARRY:, :] / cnt - pg
        ys.append(_dot(pooled.astype(BF16), wpool_ref[gi]))
    cp[...] = p[TM - POOL_CARRY:, :]
    yb = jnp.concatenate(ys, axis=-1)
    b_out[0] = ((1.0 + jnp.tanh(gbbuf[...])) * yb * ps_ref[...]).astype(BF16)

    lane = lax.broadcasted_iota(jnp.int32, (1, LANES), 1)
    valid = pos >= 0
    beta = jnp.where(valid, _sigmoid(ba), 0.0)
    g = jnp.where(valid, -jnp.exp(avec_ref[...]) * _softplus(ba + dtvec_ref[...]), 0.0)
    g = jnp.where((lane >= HEADS) & (lane < 2 * HEADS), g, 0.0)
    g_hi = g.astype(BF16)
    g_lo = (g - g_hi.astype(F32)).astype(BF16)
    tri = tri_ref[...]
    gc = _dot(tri, g_hi) + _dot(tri, g_lo)
    col = jnp.where(lane < HEADS, beta, gc)
    gcol_out[0] = col
    rowform = col.T
    for c in range(TM // CHUNK):
        cs = slice(c * CHUNK, (c + 1) * CHUNK)
        grow_out[0, c] = jnp.concatenate(
            [jnp.concatenate([rowform[r0 + h:r0 + h + 1, cs] for h in range(HEADS)], axis=1)
             for r0 in (0, HEADS)], axis=0)


def _pair_blocks(x):
    a, b = x[:, :HEAD_DIM], x[:, HEAD_DIM:]
    z = jnp.zeros_like(a)
    return jnp.concatenate([jnp.concatenate([a, z], axis=1), jnp.concatenate([z, b], axis=1)], axis=0)


def _delta_kernel(q_ref, k_ref, v_ref, gcol_ref, grow_ref, o_ref,
                  state, tb_s, tbe_s, qk_s, deg_s, kdec_s, gtot_s):
    i = pl.program_id(0)
    nb = q_ref.shape[0]
    gw = GROUP * CHUNK

    @pl.when(i == 0)
    def _():
        state[...] = jnp.zeros_like(state)

    ri = lax.broadcasted_iota(jnp.int32, (CHUNK, gw), 0)
    ci = lax.broadcasted_iota(jnp.int32, (CHUNK, gw), 1) & (CHUNK - 1)
    causal = ri >= ci
    strict = ri > ci
    eye = ri == ci
    bd_mask = ((lax.broadcasted_iota(jnp.int32, (gw, gw), 0) >> 6)
               == (lax.broadcasted_iota(jnp.int32, (gw, gw), 1) >> 6))
    lane_half = lax.broadcasted_iota(jnp.int32, (CHUNK, LANES), 1) >> 6
    zblk = jnp.zeros((CHUNK, HEAD_DIM), BF16)

    def block_diag(pb):
        return jnp.where(bd_mask, jnp.concatenate([pb] * GROUP, axis=0), jnp.zeros((), BF16))

    def lane_bcast(col, idx):
        return jnp.take_along_axis(col, idx, axis=1)

    def phase_a(j, carry):
        xs, ps, meta = [], [], []
        for t in range(A_CHUNKS):
            c = j * A_CHUNKS + t
            rows = pl.ds(pl.multiple_of(c * CHUNK, CHUNK), CHUNK)
            for b in range(nb):
                col = gcol_ref[b, rows, :]
                rowc = grow_ref[b, c]
                beta_r, gc_r = rowc[0:1, :], rowc[1:2, :]
                beta_c = jnp.concatenate(
                    [lane_bcast(col, 2 * p + lane_half) for p in range(HEADS // 2)], axis=1)
                gc_c = jnp.concatenate(
                    [lane_bcast(col, HEADS + 2 * p + lane_half) for p in range(HEADS // 2)], axis=1)
                last = col[CHUNK - 1:CHUNK, :]
                ek_col = jnp.exp(last - col)
                gtot = jnp.exp(last)
                k_all = k_ref[b, rows, :]
                q_all = q_ref[b, rows, :]
                kdec, gts = [], []
                for h in range(HEADS):
                    hs = slice(h * HEAD_DIM, (h + 1) * HEAD_DIM)
                    ek_b = lane_bcast(ek_col, jnp.full((CHUNK, LANES), HEADS + h, jnp.int32))
                    kdec.append((k_all[:, hs].astype(F32) * ek_b).astype(BF16))
                    gts.append(jnp.broadcast_to(gtot[:, HEADS + h:HEADS + h + 1], (SUBLANES, HEAD_DIM)))
                kdec_s[b, rows, :] = jnp.concatenate(kdec, axis=1)
                gtot_s[b, c] = jnp.concatenate(gts, axis=1)
                for g in range(HEADS // GROUP):
                    gl = slice(g * GROUP * HEAD_DIM, (g + 1) * GROUP * HEAD_DIM)
                    cb = slice(g * gw, (g + 1) * gw)
                    kg = k_all[:, gl]
                    lhs = jnp.concatenate([kg, q_all[:, gl]], axis=0)
                    rhs_t = jnp.concatenate(
                        [jnp.concatenate([kg[:, m * HEAD_DIM:(m + 1) * HEAD_DIM] if m == h else zblk
                                          for m in range(GROUP)], axis=1) for h in range(GROUP)], axis=0)
                    sc = _dot_nt(lhs, rhs_t)
                    decay = jnp.exp(jnp.minimum(gc_c[:, cb] - gc_r[:, cb], 0.0))
                    p0 = jnp.where(strict, -(sc[:CHUNK] * beta_c[:, cb] * decay), 0.0)
                    qk_s[b, c, :, cb] = jnp.where(causal, sc[CHUNK:] * decay, 0.0).astype(BF16)
                    eg_r = jnp.exp(gc_r[:, cb])
                    deg_s[b, c, :, cb] = jnp.where(eye, eg_r, 0.0).astype(BF16)
                    xs.append(jnp.where(eye, 1.0, 0.0) + p0)
                    ps.append(p0)
                    meta.append((b, c, cb, beta_r[:, cb], eg_r))
        for s in range(6):
            for u in range(len(xs)):
                pb = ps[u].astype(BF16)
                bd = block_diag(pb)
                if s == 0:
                    ps[u] = _dot(pb, bd)
                elif s < 5:
                    r = _dot(jnp.concatenate([xs[u].astype(BF16), pb], axis=0), bd)
                    xs[u] = xs[u] + r[:CHUNK]
                    ps[u] = r[CHUNK:]
                else:
                    xs[u] = xs[u] + _dot(xs[u].astype(BF16), bd)
        for u, (b, c, cb, beta_r, eg_r) in enumerate(meta):
            tb = xs[u] * beta_r
            tb_s[b, c, :, cb] = tb.astype(BF16)
            tbe_s[b, c, :, cb] = (-(tb * eg_r)).astype(BF16)
        return carry

    lax.fori_loop(0, (TM // CHUNK) // A_CHUNKS, phase_a, 0)

    units = [(b, p) for b in range(nb) for p in range(HEADS // 2)]
    for c in range(TM // CHUNK):
        rows = slice(c * CHUNK, (c + 1) * CHUNK)
        r1s, vns = [], []
        for b, p in units:
            pw = slice(p * 2 * HEAD_DIM, (p + 1) * 2 * HEAD_DIM)
            kq = jnp.concatenate([k_ref[b, rows, pw], q_ref[b, rows, pw]], axis=0)
            r1s.append(_dot(kq, _pair_blocks(state[b, p].astype(BF16))))
        for u, (b, p) in enumerate(units):
            pw = slice(p * 2 * HEAD_DIM, (p + 1) * 2 * HEAD_DIM)
            pc = slice(p * 2 * CHUNK, (p + 1) * 2 * CHUNK)
            ks = r1s[u][:CHUNK].astype(BF16)
            rhs = jnp.concatenate([_pair_blocks(v_ref[b, rows, pw]), _pair_blocks(ks)], axis=0)
            lhs = jnp.concatenate([tb_s[b, c, :, pc], tbe_s[b, c, :, pc]], axis=1)
            vns.append(_pair_blocks(_dot(lhs, rhs).astype(BF16)))
        for u, (b, p) in enumerate(units):
            pw = slice(p * 2 * HEAD_DIM, (p + 1) * 2 * HEAD_DIM)
            kd = kdec_s[b, rows, pw]
            ds = _dot_tn(jnp.concatenate([kd[:, :HEAD_DIM], kd[:, HEAD_DIM:]], axis=0), vns[u])
            state[b, p] = state[b, p] * gtot_s[b, c][0:1, pw] + ds
        for u, (b, p) in enumerate(units):
            pw = slice(p * 2 * HEAD_DIM, (p + 1) * 2 * HEAD_DIM)
            pc = slice(p * 2 * CHUNK, (p + 1) * 2 * CHUNK)
            qs = r1s[u][CHUNK:].astype(BF16)
            rhs = jnp.concatenate([vns[u], _pair_blocks(qs)], axis=0)
            lhs = jnp.concatenate([qk_s[b, c, :, pc], deg_s[b, c, :, pc]], axis=1)
            o_ref[b, rows, pw] = _dot(lhs, rhs)


def _back_kernel(o_ref, a_ref, b_ref, *refs, from_tokens, final):
    h_refs, refs = _split_stream(refs, from_tokens)
    (wout_ref, gain_ref, wup_ref, cw_ref, wdown_ref, gfin_ref, out_ref, cg, cv) = refs
    i = pl.program_id(1)

    @pl.when(i == 0)
    def _():
        zeros = jnp.zeros((SUBLANES, D_FF), F32)
        cg[0:SUBLANES, :] = zeros
        cv[0:SUBLANES, :] = zeros

    ys = []
    for h in range(HEADS):
        hs = slice(h * HEAD_DIM, (h + 1) * HEAD_DIM)
        oh = o_ref[0, :, hs]
        ms = jnp.mean(oh * oh, axis=-1, keepdims=True)
        ys.append(oh * lax.rsqrt(ms + NORM_EPS) * a_ref[0, :, hs].astype(F32)
                  + b_ref[0, :, hs].astype(F32))
    y = jnp.concatenate(ys, axis=-1).astype(BF16)
    h1 = _stream_tile(i, h_refs) + _dot(y, wout_ref[...])

    un = _rms(h1, gain_ref[...]).astype(BF16)
    cw = cw_ref[...]
    slabs = [slice(lo, hi) for lo, hi in zip(FF_SPLITS[:-1], FF_SPLITS[1:])]

    def shifted(cs):
        return slice(D_FF + cs.start, D_FF + cs.stop)

    def up(cs):
        cg[SUBLANES:, cs] = _dot(un, wup_ref[:, cs])
        cv[SUBLANES:, cs] = _dot(un, wup_ref[:, shifted(cs)])

    for cs in slabs[:2]:
        up(cs)
    h2 = h1
    for s, cs in enumerate(slabs):
        half_gate = _causal_conv(cg, cs, cw[:, cs], FFN_CONV)
        val = _causal_conv(cv, cs, cw[:, shifted(cs)], FFN_CONV)
        act = (_silu_of_twice(half_gate) * val).astype(BF16)
        if s + 2 < len(slabs):
            up(slabs[s + 2])
        h2 = h2 + _dot(act, wdown_ref[cs, :])
    if final:
        h2 = _rms(h2, gfin_ref[...])
    out_ref[0] = h2


def _resident(shape):
    nd = len(shape)
    return pl.BlockSpec(shape, lambda b, i: (0,) * nd, pipeline_mode=pl.Buffered(1))


def _tile(width):
    return pl.BlockSpec((1, TM, width), lambda b, i: (b, i, 0))


def _params():
    return pltpu.CompilerParams(dimension_semantics=("arbitrary", "arbitrary"),
                                vmem_limit_bytes=VMEM_LIMIT)


def _token_tile():
    return pl.BlockSpec((1, TM, D_MODEL), lambda b, i: (b, jnp.maximum(i - 1, 0), 0))


def _stream_specs(stream):
    if len(stream) == 1:
        return [_tile(D_MODEL)]
    return [_token_tile(), _resident(stream[1].shape)]


def _layer_resident(stacked, layer):
    rest = stacked.shape[1:]
    return pl.BlockSpec((None,) + rest, lambda b, i: (layer,) + (0,) * len(rest),
                        pipeline_mode=pl.Buffered(1))


def _front(stream, t, layer, stacked, tri):
    bsz = stream[0].shape[0]
    nt = t // TM
    nc = TM // CHUNK
    act = jax.ShapeDtypeStruct((bsz, t, D_MODEL), BF16)
    return pl.pallas_call(
        functools.partial(_front_kernel, from_tokens=len(stream) == 2),
        grid=(bsz, nt),
        in_specs=_stream_specs(stream) + [_layer_resident(s, layer) for s in stacked]
        + [_resident(tri.shape)],
        out_specs=[_tile(D_MODEL)] * 5 + [
            _tile(LANES),
            pl.BlockSpec((1, nc, 2, CAT), lambda b, i: (b, i, 0, 0))],
        out_shape=[act] * 5 + [
            jax.ShapeDtypeStruct((bsz, t, LANES), F32),
            jax.ShapeDtypeStruct((bsz, t // CHUNK, 2, CAT), F32)],
        scratch_shapes=[pltpu.VMEM((SUBLANES + TM, D_MODEL), F32)] * 3
        + [pltpu.VMEM((POOL_CARRY, POOL_WIDTH), F32)]
        + [pltpu.VMEM((TM, D_MODEL), F32)] * 3
        + [pltpu.VMEM((TM, POOL_WIDTH + LANES), F32)],
        compiler_params=_params(),
        name="front",
    )(*stream, *stacked, tri)


def _delta(q, k, v, gcol, grow):
    bsz, t, _ = q.shape
    nt = t // TM
    nc = TM // CHUNK
    def both(width):
        return pl.BlockSpec((bsz, TM, width), lambda i: (0, i, 0))

    chunk_mats = pltpu.VMEM((bsz, nc, CHUNK, CAT), BF16)
    return pl.pallas_call(
        _delta_kernel,
        grid=(nt,),
        in_specs=[both(D_MODEL)] * 3 + [
            both(LANES),
            pl.BlockSpec((bsz, nc, 2, CAT), lambda i: (0, i, 0, 0))],
        out_specs=both(D_MODEL),
        out_shape=jax.ShapeDtypeStruct((bsz, t, D_MODEL), F32),
        scratch_shapes=[pltpu.VMEM((bsz, HEADS // 2, HEAD_DIM, 2 * HEAD_DIM), F32)]
        + [chunk_mats] * 4
        + [pltpu.VMEM((bsz, TM, D_MODEL), BF16),
           pltpu.VMEM((bsz, nc, SUBLANES, D_MODEL), F32)],
        compiler_params=pltpu.CompilerParams(dimension_semantics=("arbitrary",),
                                             vmem_limit_bytes=VMEM_LIMIT),
        name="delta",
    )(q, k, v, gcol, grow)


def _back(o, a, b, stream, layer, stacked, gfin, final):
    bsz, t, _ = o.shape
    nt = t // TM
    if final:
        out_spec, out_len = _token_tile(), t - TM
    else:
        out_spec, out_len = _tile(D_MODEL), t
    return pl.pallas_call(
        functools.partial(_back_kernel, from_tokens=len(stream) == 2, final=final),
        grid=(bsz, nt),
        in_specs=[_tile(D_MODEL)] * 3 + _stream_specs(stream)
        + [_layer_resident(s, layer) for s in stacked] + [_resident(gfin.shape)],
        out_specs=out_spec,
        out_shape=jax.ShapeDtypeStruct((bsz, out_len, D_MODEL), F32),
        scratch_shapes=[pltpu.VMEM((SUBLANES + TM, D_FF), F32)] * 2,
        compiler_params=_params(),
        name="back",
    )(o, a, b, *stream, *stacked, gfin)


def _block_tri():
    r = jnp.arange(TM)
    same_chunk = (r[:, None] // CHUNK) == (r[None, :] // CHUNK)
    return (same_chunk & (r[:, None] >= r[None, :])).astype(BF16)


def _lane_vecs(vals, offset):
    depth, n = vals.shape
    return jnp.zeros((depth, 1, LANES), F32).at[:, 0, offset:offset + n].set(vals.astype(F32))


def _pack_w_in(w_in):
    depth = w_in.shape[0]
    o_z, o_b, o_pool, o_gate = QKV_DIM, QKV_DIM + D_MODEL, QKV_DIM + D_MODEL + 2 * HEADS, \
        QKV_DIM + D_MODEL + 2 * HEADS + POOL_WIDTH
    parts = [w_in[..., :o_z], 0.5 * w_in[..., o_z:o_b], 0.5 * w_in[..., o_gate:],
             w_in[..., o_pool:o_gate], w_in[..., o_b:o_pool],
             jnp.zeros((depth, D_MODEL, LANES - 2 * HEADS), w_in.dtype)]
    return jnp.concatenate(parts, axis=-1).astype(BF16)


def kernel(x, meta_tokens, norm_mix, w_in, conv_qkv, a_log, dt_bias, head_norm, w_pool,
           pool_scale, w_out, norm_ffn, w_up, conv_ffn, w_down, norm_final):
    bsz, seq, _ = x.shape
    depth = w_in.shape[0]
    t = TM + seq
    first = jnp.concatenate([jnp.zeros((LEAD, D_MODEL), x.dtype), meta_tokens.astype(x.dtype)], axis=0)
    stream = (x, first)
    tri = _block_tri()
    front_params = (
        norm_mix[:, None, :], _pack_w_in(w_in), 0.5 * conv_qkv,
        _lane_vecs(a_log, HEADS), _lane_vecs(dt_bias, HEADS),
        0.5 * jnp.tile(head_norm, (1, HEADS))[:, None, :], w_pool.astype(BF16),
        0.5 * pool_scale[:, None, :])
    ffn_taps = jnp.concatenate([0.5 * conv_ffn[..., :D_FF], conv_ffn[..., D_FF:]], axis=-1)
    back_params = (w_out.astype(BF16), norm_ffn[:, None, :], w_up.astype(BF16), ffn_taps,
                   w_down.astype(BF16))
    for layer in range(depth):
        q, k, v, a, b, gcol, grow = _front(stream, t, layer, front_params, tri)
        o = _delta(q, k, v, gcol, grow)
        h = _back(o, a, b, stream, layer, back_params, norm_final[None, :], layer == depth - 1)
        stream = (h,)
    return h
```

```python
import functools

import jax
import jax.numpy as jnp
from jax import lax
from jax.experimental import pallas as pl
from jax.experimental.pallas import tpu as pltpu

D_MODEL = 1024
N_META = 16
HEADS = 8
HEAD_DIM = 128
QKV_DIM = 3 * D_MODEL
DN_CONV = 4
CHUNK = 64
POOL_WINDOWS = (2, 4, 8, 16)
POOL_GROUP_DIM = 128
POOL_WIDTH = 512
POOL_OUT_GROUP = 256
D_FF = 2816
FFN_CONV = 3
NORM_EPS = 1e-6

TM = 512
LEAD = TM - N_META
LANES = 128
SUBLANES = 8
POOL_CARRY = 16
GROUP = 4
CAT = HEADS * CHUNK
A_CHUNKS = 4
MXU_TILE = 256
FF_SPLITS = tuple(range(0, D_FF, 3 * MXU_TILE)) + (D_FF,)
VMEM_LIMIT = 56 * 1024 * 1024

W_IN_COLS = {"q": (0, 0, D_MODEL), "k": (0, D_MODEL, 2 * D_MODEL), "v": (0, 2 * D_MODEL, QKV_DIM),
             "z": (1, 0, D_MODEL), "gate_a": (1, D_MODEL, 2 * D_MODEL),
             "gate_b": (1, 2 * D_MODEL, 3 * D_MODEL),
             "pool_ba": (1, 3 * D_MODEL, 3 * D_MODEL + POOL_WIDTH + LANES)}

F32 = jnp.float32
BF16 = jnp.bfloat16


def _sigmoid(x):
    return 0.5 * jnp.tanh(0.5 * x) + 0.5


def _silu_of_twice(half):
    return half + half * jnp.tanh(half)


def _softplus(x):
    return jnp.maximum(x, 0.0) + jnp.log(1.0 + jnp.exp(-jnp.abs(x)))


def _dot(a, b):
    return jnp.dot(a, b, preferred_element_type=F32)


def _dot_nt(a, b):
    return lax.dot_general(a, b, (((1,), (1,)), ((), ())), preferred_element_type=F32)


def _dot_tn(a, b):
    return lax.dot_general(a, b, (((0,), (0,)), ((), ())), preferred_element_type=F32)


def _rms(x, gain):
    ms = jnp.mean(x * x, axis=-1, keepdims=True)
    return x * lax.rsqrt(ms + NORM_EPS) * gain


def _causal_conv(buf_ref, cs, cw, width):
    rows = buf_ref.shape[0] - SUBLANES
    groups = rows // SUBLANES
    pre = buf_ref[SUBLANES:, cs]
    cols = pre.shape[1]
    cur = pre.reshape(groups, SUBLANES, cols)
    prev = buf_ref[0:rows, cs].reshape(groups, SUBLANES, cols)
    sub = lax.broadcasted_iota(jnp.int32, (1, SUBLANES, cols), 1)
    acc = pre * cw[width - 1:width, :]
    for j in range(1, width):
        picked = jnp.where(sub < SUBLANES - j, cur, prev)
        shifted = pltpu.roll(picked, j, axis=1).reshape(rows, cols)
        acc = acc + shifted * cw[width - 1 - j:width - j, :]
    buf_ref[0:SUBLANES, cs] = pre[rows - SUBLANES:, :]
    return acc


def _split_stream(refs, from_tokens):
    n = 2 if from_tokens else 1
    return refs[:n], refs[n:]


def _stream_tile(i, h_refs):
    if len(h_refs) == 1:
        return h_refs[0][0]
    tok_ref, first_ref = h_refs
    return jnp.where(i == 0, first_ref[...], tok_ref[0])


def _front_kernel(*refs, from_tokens):
    h_refs, refs = _split_stream(refs, from_tokens)
    (gain_ref, wqkv_ref, wrest_ref, cw_ref, avec_ref, dtvec_ref, hn_ref, wpool_ref, ps_ref, tri_ref,
     q_out, k_out, v_out, a_out, b_out, gcol_out, grow_out,
     cq, ck, cv, cp, zbuf, gabuf, gbbuf, pbbuf) = refs
    i = pl.program_id(1)

    @pl.when(i == 0)
    def _():
        zeros = jnp.zeros((SUBLANES, D_MODEL), F32)
        cq[0:SUBLANES, :] = zeros
        ck[0:SUBLANES, :] = zeros
        cv[0:SUBLANES, :] = zeros
        cp[...] = jnp.zeros_like(cp)

    x = _stream_tile(i, h_refs)
    xn = _rms(x, gain_ref[...]).astype(BF16)
    cw = cw_ref[...]
    full = slice(0, D_MODEL)

    def proj(name):
        which, lo, hi = W_IN_COLS[name]
        return _dot(xn, (wqkv_ref, wrest_ref)[which][:, lo:hi])

    cq[SUBLANES:, :] = proj("q")
    ck[SUBLANES:, :] = proj("k")
    cv[SUBLANES:, :] = proj("v")
    gabuf[...] = proj("gate_a")
    zbuf[...] = proj("z")
    pbbuf[...] = proj("pool_ba")
    gbbuf[...] = proj("gate_b")

    def qkv_section(buf_ref, sec):
        return _silu_of_twice(
            _causal_conv(buf_ref, full, cw[:, sec * D_MODEL:(sec + 1) * D_MODEL], DN_CONV))

    def l2norm_store(xs, out_ref, scale):
        for h in range(HEADS):
            hs = slice(h * HEAD_DIM, (h + 1) * HEAD_DIM)
            xh = xs[:, hs]
            ss = jnp.sum(xh * xh, axis=-1, keepdims=True)
            out_ref[0, :, hs] = (xh * (lax.rsqrt(ss + NORM_EPS) * scale)).astype(BF16)

    l2norm_store(qkv_section(cq, 0), q_out, HEAD_DIM ** -0.5)
    l2norm_store(qkv_section(ck, 1), k_out, 1.0)
    v_out[0] = qkv_section(cv, 2).astype(BF16)
    a_out[0] = (hn_ref[...] * _silu_of_twice(zbuf[...]) * (1.0 + jnp.tanh(gabuf[...]))).astype(BF16)

    p = pbbuf[:, :POOL_WIDTH]
    ba = pbbuf[:, POOL_WIDTH:POOL_WIDTH + LANES]
    row = lax.broadcasted_iota(jnp.int32, (TM, 1), 0)
    pos = i * TM + row - LEAD
    ys = []
    for gi, win in enumerate(POOL_WINDOWS):
        gs = slice(gi * POOL_GROUP_DIM, (gi + 1) * POOL_GROUP_DIM)
        pg = p[:, gs]
        s = jnp.concatenate([cp[:, gs], pg], axis=0)
        sh = 1
        while sh < win:
            s = s + pltpu.roll(s, sh, axis=0)
            sh *= 2
        cnt = jnp.clip(pos + 1, 1, win).astype(F32)
        pooled = s[POOL_CARRY:, :] / cnt - pg
        ys.append(_dot(pooled.astype(BF16), wpool_ref[gi]))
    cp[...] = p[TM - POOL_CARRY:, :]
    yb = jnp.concatenate(ys, axis=-1)
    b_out[0] = ((1.0 + jnp.tanh(gbbuf[...])) * yb * ps_ref[...]).astype(BF16)

    lane = lax.broadcasted_iota(jnp.int32, (1, LANES), 1)
    valid = pos >= 0
    beta = jnp.where(valid, _sigmoid(ba), 0.0)
    g = jnp.where(valid, -jnp.exp(avec_ref[...]) * _softplus(ba + dtvec_ref[...]), 0.0)
    g = jnp.where((lane >= HEADS) & (lane < 2 * HEADS), g, 0.0)
    g_hi = g.astype(BF16)
    g_lo = (g - g_hi.astype(F32)).astype(BF16)
    tri = tri_ref[...]
    gc = _dot(tri, g_hi) + _dot(tri, g_lo)
    col = jnp.where(lane < HEADS, beta, gc)
    gcol_out[0] = col
    rowform = col.T
    for c in range(TM // CHUNK):
        cs = slice(c * CHUNK, (c + 1) * CHUNK)
        grow_out[0, c] = jnp.concatenate(
            [jnp.concatenate([rowform[r0 + h:r0 + h + 1, cs] for h in range(HEADS)], axis=1)
             for r0 in (0, HEADS)], axis=0)


def _pair_blocks(x):
    a, b = x[:, :HEAD_DIM], x[:, HEAD_DIM:]
    z = jnp.zeros_like(a)
    return jnp.concatenate([jnp.concatenate([a, z], axis=1), jnp.concatenate([z, b], axis=1)], axis=0)


def _delta_kernel(q_ref, k_ref, v_ref, gcol_ref, grow_ref, o_ref,
                  state, tb_s, tbe_s, qk_s, deg_s, kdec_s, gtot_s):
    i = pl.program_id(0)
    nb = q_ref.shape[0]
    gw = GROUP * CHUNK

    @pl.when(i == 0)
    def _():
        state[...] = jnp.zeros_like(state)

    ri = lax.broadcasted_iota(jnp.int32, (CHUNK, gw), 0)
    ci = lax.broadcasted_iota(jnp.int32, (CHUNK, gw), 1) & (CHUNK - 1)
    causal = ri >= ci
    strict = ri > ci
    eye = ri == ci
    bd_mask = ((lax.broadcasted_iota(jnp.int32, (gw, gw), 0) >> 6)
               == (lax.broadcasted_iota(jnp.int32, (gw, gw), 1) >> 6))
    lane_half = lax.broadcasted_iota(jnp.int32, (CHUNK, LANES), 1) >> 6
    zblk = jnp.zeros((CHUNK, HEAD_DIM), BF16)

    def block_diag(pb):
        return jnp.where(bd_mask, jnp.concatenate([pb] * GROUP, axis=0), jnp.zeros((), BF16))

    def lane_bcast(col, idx):
        return jnp.take_along_axis(col, idx, axis=1)

    def phase_a(j, carry):
        xs, ps, meta = [], [], []
        for t in range(A_CHUNKS):
            c = j * A_CHUNKS + t
            rows = pl.ds(pl.multiple_of(c * CHUNK, CHUNK), CHUNK)
            for b in range(nb):
                col = gcol_ref[b, rows, :]
                rowc = grow_ref[b, c]
                beta_r, gc_r = rowc[0:1, :], rowc[1:2, :]
                beta_c = jnp.concatenate(
                    [lane_bcast(col, 2 * p + lane_half) for p in range(HEADS // 2)], axis=1)
                gc_c = jnp.concatenate(
                    [lane_bcast(col, HEADS + 2 * p + lane_half) for p in range(HEADS // 2)], axis=1)
                last = col[CHUNK - 1:CHUNK, :]
                ek_col = jnp.exp(last - col)
                gtot = jnp.exp(last)
                k_all = k_ref[b, rows, :]
                q_all = q_ref[b, rows, :]
                kdec, gts = [], []
                for h in range(HEADS):
                    hs = slice(h * HEAD_DIM, (h + 1) * HEAD_DIM)
                    ek_b = lane_bcast(ek_col, jnp.full((CHUNK, LANES), HEADS + h, jnp.int32))
                    kdec.append((k_all[:, hs].astype(F32) * ek_b).astype(BF16))
                    gts.append(jnp.broadcast_to(gtot[:, HEADS + h:HEADS + h + 1], (SUBLANES, HEAD_DIM)))
                kdec_s[b, rows, :] = jnp.concatenate(kdec, axis=1)
                gtot_s[b, c] = jnp.concatenate(gts, axis=1)
                for g in range(HEADS // GROUP):
                    gl = slice(g * GROUP * HEAD_DIM, (g + 1) * GROUP * HEAD_DIM)
                    cb = slice(g * gw, (g + 1) * gw)
                    kg = k_all[:, gl]
                    lhs = jnp.concatenate([kg, q_all[:, gl]], axis=0)
                    rhs_t = jnp.concatenate(
                        [jnp.concatenate([kg[:, m * HEAD_DIM:(m + 1) * HEAD_DIM] if m == h else zblk
                                          for m in range(GROUP)], axis=1) for h in range(GROUP)], axis=0)
                    sc = _dot_nt(lhs, rhs_t)
                    decay = jnp.exp(jnp.minimum(gc_c[:, cb] - gc_r[:, cb], 0.0))
                    p0 = jnp.where(strict, -(sc[:CHUNK] * beta_c[:, cb] * decay), 0.0)
                    qk_s[b, c, :, cb] = jnp.where(causal, sc[CHUNK:] * decay, 0.0).astype(BF16)
                    eg_r = jnp.exp(gc_r[:, cb])
                    deg_s[b, c, :, cb] = jnp.where(eye, eg_r, 0.0).astype(BF16)
                    xs.append(jnp.where(eye, 1.0, 0.0) + p0)
                    ps.append(p0)
                    meta.append((b, c, cb, beta_r[:, cb], eg_r))
        for s in range(6):
            for u in range(len(xs)):
                pb = ps[u].astype(BF16)
                bd = block_diag(pb)
                if s == 0:
                    ps[u] = _dot(pb, bd)
                elif s < 5:
                    r = _dot(jnp.concatenate([xs[u].astype(BF16), pb], axis=0), bd)
                    xs[u] = xs[u] + r[:CHUNK]
                    ps[u] = r[CHUNK:]
                else:
                    xs[u] = xs[u] + _dot(xs[u].astype(BF16), bd)
        for u, (b, c, cb, beta_r, eg_r) in enumerate(meta):
            tb = xs[u] * beta_r
            tb_s[b, c, :, cb] = tb.astype(BF16)
            tbe_s[b, c, :, cb] = (-(tb * eg_r)).astype(BF16)
        return carry

    lax.fori_loop(0, (TM // CHUNK) // A_CHUNKS, phase_a, 0)

    units = [(b, p) for b in range(nb) for p in range(HEADS // 2)]
    for c in range(TM // CHUNK):
        rows = slice(c * CHUNK, (c + 1) * CHUNK)
        r1s, vns = [], []
        for b, p in units:
            pw = slice(p * 2 * HEAD_DIM, (p + 1) * 2 * HEAD_DIM)
            kq = jnp.concatenate([k_ref[b, rows, pw], q_ref[b, rows, pw]], axis=0)
            r1s.append(_dot(kq, _pair_blocks(state[b, p].astype(BF16))))
        for u, (b, p) in enumerate(units):
            pw = slice(p * 2 * HEAD_DIM, (p + 1) * 2 * HEAD_DIM)
            pc = slice(p * 2 * CHUNK, (p + 1) * 2 * CHUNK)
            ks = r1s[u][:CHUNK].astype(BF16)
            rhs = jnp.concatenate([_pair_blocks(v_ref[b, rows, pw]), _pair_blocks(ks)], axis=0)
            lhs = jnp.concatenate([tb_s[b, c, :, pc], tbe_s[b, c, :, pc]], axis=1)
            vns.append(_pair_blocks(_dot(lhs, rhs).astype(BF16)))
        for u, (b, p) in enumerate(units):
            pw = slice(p * 2 * HEAD_DIM, (p + 1) * 2 * HEAD_DIM)
            kd = kdec_s[b, rows, pw]
            ds = _dot_tn(jnp.concatenate([kd[:, :HEAD_DIM], kd[:, HEAD_DIM:]], axis=0), vns[u])
            state[b, p] = state[b, p] * gtot_s[b, c][0:1, pw] + ds
        for u, (b, p) in enumerate(units):
            pw = slice(p * 2 * HEAD_DIM, (p + 1) * 2 * HEAD_DIM)
            pc = slice(p * 2 * CHUNK, (p + 1) * 2 * CHUNK)
            qs = r1s[u][CHUNK:].astype(BF16)
            rhs = jnp.concatenate([vns[u], _pair_blocks(qs)], axis=0)
            lhs = jnp.concatenate([qk_s[b, c, :, pc], deg_s[b, c, :, pc]], axis=1)
            o_ref[b, rows, pw] = _dot(lhs, rhs)


def _back_kernel(o_ref, a_ref, b_ref, *refs, from_tokens, final):
    h_refs, refs = _split_stream(refs, from_tokens)
    (wout_ref, gain_ref, wup_ref, cw_ref, wdown_ref, gfin_ref, out_ref, cg, cv) = refs
    i = pl.program_id(1)

    @pl.when(i == 0)
    def _():
        zeros = jnp.zeros((SUBLANES, D_FF), F32)
        cg[0:SUBLANES, :] = zeros
        cv[0:SUBLANES, :] = zeros

    ys = []
    for h in range(HEADS):
        hs = slice(h * HEAD_DIM, (h + 1) * HEAD_DIM)
        oh = o_ref[0, :, hs]
        ms = jnp.mean(oh * oh, axis=-1, keepdims=True)
        ys.append(oh * lax.rsqrt(ms + NORM_EPS) * a_ref[0, :, hs].astype(F32)
                  + b_ref[0, :, hs].astype(F32))
    y = jnp.concatenate(ys, axis=-1).astype(BF16)
    h1 = _stream_tile(i, h_refs) + _dot(y, wout_ref[...])

    un = _rms(h1, gain_ref[...]).astype(BF16)
    cw = cw_ref[...]
    slabs = [slice(lo, hi) for lo, hi in zip(FF_SPLITS[:-1], FF_SPLITS[1:])]

    def shifted(cs):
        return slice(D_FF + cs.start, D_FF + cs.stop)

    def up(cs):
        cg[SUBLANES:, cs] = _dot(un, wup_ref[:, cs])
        cv[SUBLANES:, cs] = _dot(un, wup_ref[:, shifted(cs)])

    for cs in slabs[:2]:
        up(cs)
    h2 = h1
    for s, cs in enumerate(slabs):
        half_gate = _causal_conv(cg, cs, cw[:, cs], FFN_CONV)
        val = _causal_conv(cv, cs, cw[:, shifted(cs)], FFN_CONV)
        act = (_silu_of_twice(half_gate) * val).astype(BF16)
        if s + 2 < len(slabs):
            up(slabs[s + 2])
        h2 = h2 + _dot(act, wdown_ref[cs, :])
    if final:
        h2 = _rms(h2, gfin_ref[...])
    out_ref[0] = h2


def _resident(shape):
    nd = len(shape)
    return pl.BlockSpec(shape, lambda b, i: (0,) * nd, pipeline_mode=pl.Buffered(1))


def _tile(width):
    return pl.BlockSpec((1, TM, width), lambda b, i: (b, i, 0))


def _params():
    return pltpu.CompilerParams(dimension_semantics=("arbitrary", "arbitrary"),
                                vmem_limit_bytes=VMEM_LIMIT)


def _token_tile():
    return pl.BlockSpec((1, TM, D_MODEL), lambda b, i: (b, jnp.maximum(i - 1, 0), 0))


def _stream_specs(stream):
    if len(stream) == 1:
        return [_tile(D_MODEL)]
    return [_token_tile(), _resident(stream[1].shape)]


def _layer_resident(stacked, layer):
    rest = stacked.shape[1:]
    return pl.BlockSpec((None,) + rest, lambda b, i: (layer,) + (0,) * len(rest),
                        pipeline_mode=pl.Buffered(1))


def _front(stream, t, layer, stacked, tri):
    bsz = stream[0].shape[0]
    nt = t // TM
    nc = TM // CHUNK
    act = jax.ShapeDtypeStruct((bsz, t, D_MODEL), BF16)
    return pl.pallas_call(
        functools.partial(_front_kernel, from_tokens=len(stream) == 2),
        grid=(bsz, nt),
        in_specs=_stream_specs(stream) + [_layer_resident(s, layer) for s in stacked]
        + [_resident(tri.shape)],
        out_specs=[_tile(D_MODEL)] * 5 + [
            _tile(LANES),
            pl.BlockSpec((1, nc, 2, CAT), lambda b, i: (b, i, 0, 0))],
        out_shape=[act] * 5 + [
            jax.ShapeDtypeStruct((bsz, t, LANES), F32),
            jax.ShapeDtypeStruct((bsz, t // CHUNK, 2, CAT), F32)],
        scratch_shapes=[pltpu.VMEM((SUBLANES + TM, D_MODEL), F32)] * 3
        + [pltpu.VMEM((POOL_CARRY, POOL_WIDTH), F32)]
        + [pltpu.VMEM((TM, D_MODEL), F32)] * 3
        + [pltpu.VMEM((TM, POOL_WIDTH + LANES), F32)],
        compiler_params=_params(),
        name="front",
    )(*stream, *stacked, tri)


def _delta(q, k, v, gcol, grow):
    bsz, t, _ = q.shape
    nt = t // TM
    nc = TM // CHUNK
    def both(width):
        return pl.BlockSpec((bsz, TM, width), lambda i: (0, i, 0))

    chunk_mats = pltpu.VMEM((bsz, nc, CHUNK, CAT), BF16)
    return pl.pallas_call(
        _delta_kernel,
        grid=(nt,),
        in_specs=[both(D_MODEL)] * 3 + [
            both(LANES),
            pl.BlockSpec((bsz, nc, 2, CAT), lambda i: (0, i, 0, 0))],
        out_specs=both(D_MODEL),
        out_shape=jax.ShapeDtypeStruct((bsz, t, D_MODEL), F32),
        scratch_shapes=[pltpu.VMEM((bsz, HEADS // 2, HEAD_DIM, 2 * HEAD_DIM), F32)]
        + [chunk_mats] * 4
        + [pltpu.VMEM((bsz, TM, D_MODEL), BF16),
           pltpu.VMEM((bsz, nc, SUBLANES, D_MODEL), F32)],
        compiler_params=pltpu.CompilerParams(dimension_semantics=("arbitrary",),
                                             vmem_limit_bytes=VMEM_LIMIT),
        name="delta",
    )(q, k, v, gcol, grow)


def _back(o, a, b, stream, layer, stacked, gfin, final):
    bsz, t, _ = o.shape
    nt = t // TM
    if final:
        out_spec, out_len = _token_tile(), t - TM
    else:
        out_spec, out_len = _tile(D_MODEL), t
    return pl.pallas_call(
        functools.partial(_back_kernel, from_tokens=len(stream) == 2, final=final),
        grid=(bsz, nt),
        in_specs=[_tile(D_MODEL)] * 3 + _stream_specs(stream)
        + [_layer_resident(s, layer) for s in stacked] + [_resident(gfin.shape)],
        out_specs=out_spec,
        out_shape=jax.ShapeDtypeStruct((bsz, out_len, D_MODEL), F32),
        scratch_shapes=[pltpu.VMEM((SUBLANES + TM, D_FF), F32)] * 2,
        compiler_params=_params(),
        name="back",
    )(o, a, b, *stream, *stacked, gfin)


def _block_tri():
    r = jnp.arange(TM)
    same_chunk = (r[:, None] // CHUNK) == (r[None, :] // CHUNK)
    return (same_chunk & (r[:, None] >= r[None, :])).astype(BF16)


def _lane_vecs(vals, offset):
    depth, n = vals.shape
    return jnp.zeros((depth, 1, LANES), F32).at[:, 0, offset:offset + n].set(vals.astype(F32))


def _pack_w_in(w_in):
    depth = w_in.shape[0]
    o_z, o_b, o_pool, o_gate = QKV_DIM, QKV_DIM + D_MODEL, QKV_DIM + D_MODEL + 2 * HEADS, \
        QKV_DIM + D_MODEL + 2 * HEADS + POOL_WIDTH

    def cols(lo, hi, scale=None):
        part = w_in[..., lo:hi].astype(BF16)
        return part if scale is None else part * jnp.asarray(scale, BF16)

    rest = jnp.concatenate(
        [cols(o_z, o_b, 0.5), cols(o_gate, w_in.shape[-1], 0.5), cols(o_pool, o_gate),
         cols(o_b, o_pool), jnp.zeros((depth, D_MODEL, LANES - 2 * HEADS), BF16)], axis=-1)
    return cols(0, o_z), rest


def kernel(x, meta_tokens, norm_mix, w_in, conv_qkv, a_log, dt_bias, head_norm, w_pool,
           pool_scale, w_out, norm_ffn, w_up, conv_ffn, w_down, norm_final):
    bsz, seq, _ = x.shape
    depth = w_in.shape[0]
    t = TM + seq
    first = jnp.concatenate([jnp.zeros((LEAD, D_MODEL), x.dtype), meta_tokens.astype(x.dtype)], axis=0)
    stream = (x, first)
    tri = _block_tri()
    front_params = (
        norm_mix[:, None, :], *_pack_w_in(w_in), 0.5 * conv_qkv,
        _lane_vecs(a_log, HEADS), _lane_vecs(dt_bias, HEADS),
        0.5 * jnp.tile(head_norm, (1, HEADS))[:, None, :], w_pool.astype(BF16),
        0.5 * pool_scale[:, None, :])
    ffn_taps = jnp.concatenate([0.5 * conv_ffn[..., :D_FF], conv_ffn[..., D_FF:]], axis=-1)
    back_params = (w_out.astype(BF16), norm_ffn[:, None, :], w_up.astype(BF16), ffn_taps,
                   w_down.astype(BF16))
    for layer in range(depth):
        q, k, v, a, b, gcol, grow = _front(stream, t, layer, front_params, tri)
        o = _delta(q, k, v, gcol, grow)
        h = _back(o, a, b, stream, layer, back_params, norm_final[None, :], layer == depth - 1)
        stream = (h,)
    return h
```

```python
import functools

import jax
import jax.numpy as jnp
from jax import lax
from jax.experimental import pallas as pl
from jax.experimental.pallas import tpu as pltpu

D_MODEL = 1024
N_META = 16
HEADS = 8
HEAD_DIM = 128
QKV_DIM = 3 * D_MODEL
DN_CONV = 4
CHUNK = 64
POOL_WINDOWS = (2, 4, 8, 16)
POOL_GROUP_DIM = 128
POOL_WIDTH = 512
POOL_OUT_GROUP = 256
D_FF = 2816
FFN_CONV = 3
NORM_EPS = 1e-6

TM = 512
LEAD = TM - N_META
LANES = 128
SUBLANES = 8
POOL_CARRY = 16
GROUP = 4
CAT = HEADS * CHUNK
A_CHUNKS = 4
MXU_TILE = 256
FF_SPLITS = tuple(range(0, D_FF, 3 * MXU_TILE)) + (D_FF,)
VMEM_LIMIT = 56 * 1024 * 1024

BA_PAD = LANES - 2 * HEADS
BETA_LANE = BA_PAD
DECAY_LANE = BA_PAD + HEADS
_SECTIONS = (("q", D_MODEL), ("k", D_MODEL), ("v", D_MODEL), ("z", D_MODEL),
             ("ba_pool", LANES + POOL_WIDTH), ("gate_a", D_MODEL), ("gate_b", D_MODEL))
W_IN_COLS = {}
_at = 0
for _name, _width in _SECTIONS:
    W_IN_COLS[_name] = (_at, _at + _width)
    _at += _width

F32 = jnp.float32
BF16 = jnp.bfloat16


def _sigmoid(x):
    return 0.5 * jnp.tanh(0.5 * x) + 0.5


def _silu_of_twice(half):
    return half + half * jnp.tanh(half)


def _softplus(x):
    return jnp.maximum(x, 0.0) + jnp.log(1.0 + jnp.exp(-jnp.abs(x)))


def _dot(a, b):
    return jnp.dot(a, b, preferred_element_type=F32)


def _dot_nt(a, b):
    return lax.dot_general(a, b, (((1,), (1,)), ((), ())), preferred_element_type=F32)


def _dot_tn(a, b):
    return lax.dot_general(a, b, (((0,), (0,)), ((), ())), preferred_element_type=F32)


def _rms(x, gain):
    ms = jnp.mean(x * x, axis=-1, keepdims=True)
    return x * lax.rsqrt(ms + NORM_EPS) * gain


def _causal_conv(buf_ref, cs, cw, width):
    rows = buf_ref.shape[0] - SUBLANES
    groups = rows // SUBLANES
    pre = buf_ref[SUBLANES:, cs]
    cols = pre.shape[1]
    cur = pre.reshape(groups, SUBLANES, cols)
    prev = buf_ref[0:rows, cs].reshape(groups, SUBLANES, cols)
    sub = lax.broadcasted_iota(jnp.int32, (1, SUBLANES, cols), 1)
    acc = pre * cw[width - 1:width, :]
    for j in range(1, width):
        picked = jnp.where(sub < SUBLANES - j, cur, prev)
        shifted = pltpu.roll(picked, j, axis=1).reshape(rows, cols)
        acc = acc + shifted * cw[width - 1 - j:width - j, :]
    buf_ref[0:SUBLANES, cs] = pre[rows - SUBLANES:, :]
    return acc


def _split_stream(refs, from_tokens):
    n = 2 if from_tokens else 1
    return refs[:n], refs[n:]


def _stream_tile(i, h_refs):
    if len(h_refs) == 1:
        return h_refs[0][0]
    tok_ref, first_ref = h_refs
    return jnp.where(i == 0, first_ref[...], tok_ref[0])


def _front_kernel(*refs, from_tokens):
    h_refs, refs = _split_stream(refs, from_tokens)
    (gain_ref, w_ref, cw_ref, avec_ref, dtvec_ref, hn_ref, wpool_ref, ps_ref, tri_ref,
     q_out, k_out, v_out, a_out, b_out, gcol_out, grow_out,
     cq, ck, cv, cp, zbuf, gabuf, gbbuf, pbbuf) = refs
    i = pl.program_id(1)

    @pl.when(i == 0)
    def _():
        zeros = jnp.zeros((SUBLANES, D_MODEL), F32)
        cq[0:SUBLANES, :] = zeros
        ck[0:SUBLANES, :] = zeros
        cv[0:SUBLANES, :] = zeros
        cp[...] = jnp.zeros_like(cp)

    x = _stream_tile(i, h_refs)
    xn = _rms(x, gain_ref[...]).astype(BF16)
    cw = cw_ref[...]
    full = slice(0, D_MODEL)

    def proj(name):
        lo, hi = W_IN_COLS[name]
        return _dot(xn, w_ref[:, lo:hi])

    cq[SUBLANES:, :] = proj("q")
    ck[SUBLANES:, :] = proj("k")
    cv[SUBLANES:, :] = proj("v")
    gabuf[...] = proj("gate_a")
    zbuf[...] = proj("z")
    pbbuf[...] = proj("ba_pool")
    gbbuf[...] = proj("gate_b")

    def qkv_section(buf_ref, sec):
        return _silu_of_twice(
            _causal_conv(buf_ref, full, cw[:, sec * D_MODEL:(sec + 1) * D_MODEL], DN_CONV))

    def l2norm_store(xs, out_ref, scale):
        for h in range(HEADS):
            hs = slice(h * HEAD_DIM, (h + 1) * HEAD_DIM)
            xh = xs[:, hs]
            ss = jnp.sum(xh * xh, axis=-1, keepdims=True)
            out_ref[0, :, hs] = (xh * (lax.rsqrt(ss + NORM_EPS) * scale)).astype(BF16)

    l2norm_store(qkv_section(cq, 0), q_out, HEAD_DIM ** -0.5)
    l2norm_store(qkv_section(ck, 1), k_out, 1.0)
    v_out[0] = qkv_section(cv, 2).astype(BF16)
    a_out[0] = (hn_ref[...] * _silu_of_twice(zbuf[...]) * (1.0 + jnp.tanh(gabuf[...]))).astype(BF16)

    ba = pbbuf[:, :LANES]
    p = pbbuf[:, LANES:]
    row = lax.broadcasted_iota(jnp.int32, (TM, 1), 0)
    pos = i * TM + row - LEAD
    ys = []
    for gi, win in enumerate(POOL_WINDOWS):
        gs = slice(gi * POOL_GROUP_DIM, (gi + 1) * POOL_GROUP_DIM)
        pg = p[:, gs]
        s = jnp.concatenate([cp[:, gs], pg], axis=0)
        sh = 1
        while sh < win:
            s = s + pltpu.roll(s, sh, axis=0)
            sh *= 2
        cnt = jnp.clip(pos + 1, 1, win).astype(F32)
        pooled = s[POOL_CARRY:, :] / cnt - pg
        ys.append(_dot(pooled.astype(BF16), wpool_ref[gi]))
    cp[...] = p[TM - POOL_CARRY:, :]
    yb = jnp.concatenate(ys, axis=-1)
    b_out[0] = ((1.0 + jnp.tanh(gbbuf[...])) * yb * ps_ref[...]).astype(BF16)

    lane = lax.broadcasted_iota(jnp.int32, (1, LANES), 1)
    valid = pos >= 0
    beta = jnp.where(valid, _sigmoid(ba), 0.0)
    g = jnp.where(valid, -jnp.exp(avec_ref[...]) * _softplus(ba + dtvec_ref[...]), 0.0)
    g = jnp.where(lane >= DECAY_LANE, g, 0.0)
    g_hi = g.astype(BF16)
    g_lo = (g - g_hi.astype(F32)).astype(BF16)
    tri = tri_ref[...]
    gc = _dot(tri, g_hi) + _dot(tri, g_lo)
    col = jnp.where(lane < DECAY_LANE, beta, gc)
    gcol_out[0] = col
    rowform = col.T
    for c in range(TM // CHUNK):
        cs = slice(c * CHUNK, (c + 1) * CHUNK)
        grow_out[0, c] = jnp.concatenate(
            [jnp.concatenate([rowform[r0 + h:r0 + h + 1, cs] for h in range(HEADS)], axis=1)
             for r0 in (BETA_LANE, DECAY_LANE)], axis=0)


def _pair_blocks(x):
    a, b = x[:, :HEAD_DIM], x[:, HEAD_DIM:]
    z = jnp.zeros_like(a)
    return jnp.concatenate([jnp.concatenate([a, z], axis=1), jnp.concatenate([z, b], axis=1)], axis=0)


def _delta_kernel(q_ref, k_ref, v_ref, gcol_ref, grow_ref, o_ref,
                  state, tb_s, tbe_s, qk_s, deg_s, kdec_s, gtot_s):
    i = pl.program_id(0)
    nb = q_ref.shape[0]
    gw = GROUP * CHUNK

    @pl.when(i == 0)
    def _():
        state[...] = jnp.zeros_like(state)

    ri = lax.broadcasted_iota(jnp.int32, (CHUNK, gw), 0)
    ci = lax.broadcasted_iota(jnp.int32, (CHUNK, gw), 1) & (CHUNK - 1)
    causal = ri >= ci
    strict = ri > ci
    eye = ri == ci
    bd_mask = ((lax.broadcasted_iota(jnp.int32, (gw, gw), 0) >> 6)
               == (lax.broadcasted_iota(jnp.int32, (gw, gw), 1) >> 6))
    lane_half = lax.broadcasted_iota(jnp.int32, (CHUNK, LANES), 1) >> 6
    zblk = jnp.zeros((CHUNK, HEAD_DIM), BF16)

    def block_diag(pb):
        return jnp.where(bd_mask, jnp.concatenate([pb] * GROUP, axis=0), jnp.zeros((), BF16))

    def lane_bcast(col, idx):
        return jnp.take_along_axis(col, idx, axis=1)

    def phase_a(j, carry):
        xs, ps, meta = [], [], []
        for t in range(A_CHUNKS):
            c = j * A_CHUNKS + t
            rows = pl.ds(pl.multiple_of(c * CHUNK, CHUNK), CHUNK)
            for b in range(nb):
                col = gcol_ref[b, rows, :]
                rowc = grow_ref[b, c]
                beta_r, gc_r = rowc[0:1, :], rowc[1:2, :]
                beta_c = jnp.concatenate(
                    [lane_bcast(col, BETA_LANE + 2 * p + lane_half) for p in range(HEADS // 2)], axis=1)
                gc_c = jnp.concatenate(
                    [lane_bcast(col, DECAY_LANE + 2 * p + lane_half) for p in range(HEADS // 2)], axis=1)
                last = col[CHUNK - 1:CHUNK, :]
                ek_col = jnp.exp(last - col)
                gtot = jnp.exp(last)
                k_all = k_ref[b, rows, :]
                q_all = q_ref[b, rows, :]
                kdec, gts = [], []
                for h in range(HEADS):
                    hs = slice(h * HEAD_DIM, (h + 1) * HEAD_DIM)
                    ek_b = lane_bcast(ek_col, jnp.full((CHUNK, LANES), DECAY_LANE + h, jnp.int32))
                    kdec.append((k_all[:, hs].astype(F32) * ek_b).astype(BF16))
                    gts.append(jnp.broadcast_to(gtot[:, DECAY_LANE + h:DECAY_LANE + h + 1],
                                                (SUBLANES, HEAD_DIM)))
                kdec_s[b, rows, :] = jnp.concatenate(kdec, axis=1)
                gtot_s[b, c] = jnp.concatenate(gts, axis=1)
                for g in range(HEADS // GROUP):
                    gl = slice(g * GROUP * HEAD_DIM, (g + 1) * GROUP * HEAD_DIM)
                    cb = slice(g * gw, (g + 1) * gw)
                    kg = k_all[:, gl]
                    lhs = jnp.concatenate([kg, q_all[:, gl]], axis=0)
                    rhs_t = jnp.concatenate(
                        [jnp.concatenate([kg[:, m * HEAD_DIM:(m + 1) * HEAD_DIM] if m == h else zblk
                                          for m in range(GROUP)], axis=1) for h in range(GROUP)], axis=0)
                    sc = _dot_nt(lhs, rhs_t)
                    decay = jnp.exp(jnp.minimum(gc_c[:, cb] - gc_r[:, cb], 0.0))
                    p0 = jnp.where(strict, -(sc[:CHUNK] * beta_c[:, cb] * decay), 0.0)
                    qk_s[b, c, :, cb] = jnp.where(causal, sc[CHUNK:] * decay, 0.0).astype(BF16)
                    eg_r = jnp.exp(gc_r[:, cb])
                    deg_s[b, c, :, cb] = jnp.where(eye, eg_r, 0.0).astype(BF16)
                    xs.append(jnp.where(eye, 1.0, 0.0) + p0)
                    ps.append(p0)
                    meta.append((b, c, cb, beta_r[:, cb], eg_r))
        for s in range(6):
            for u in range(len(xs)):
                pb = ps[u].astype(BF16)
                bd = block_diag(pb)
                if s == 0:
                    ps[u] = _dot(pb, bd)
                elif s < 5:
                    r = _dot(jnp.concatenate([xs[u].astype(BF16), pb], axis=0), bd)
                    xs[u] = xs[u] + r[:CHUNK]
                    ps[u] = r[CHUNK:]
                else:
                    xs[u] = xs[u] + _dot(xs[u].astype(BF16), bd)
        for u, (b, c, cb, beta_r, eg_r) in enumerate(meta):
            tb = xs[u] * beta_r
            tb_s[b, c, :, cb] = tb.astype(BF16)
            tbe_s[b, c, :, cb] = (-(tb * eg_r)).astype(BF16)
        return carry

    lax.fori_loop(0, (TM // CHUNK) // A_CHUNKS, phase_a, 0)

    units = [(b, p) for b in range(nb) for p in range(HEADS // 2)]
    for c in range(TM // CHUNK):
        rows = slice(c * CHUNK, (c + 1) * CHUNK)
        r1s, vns = [], []
        for b, p in units:
            pw = slice(p * 2 * HEAD_DIM, (p + 1) * 2 * HEAD_DIM)
            kq = jnp.concatenate([k_ref[b, rows, pw], q_ref[b, rows, pw]], axis=0)
            r1s.append(_dot(kq, _pair_blocks(state[b, p].astype(BF16))))
        for u, (b, p) in enumerate(units):
            pw = slice(p * 2 * HEAD_DIM, (p + 1) * 2 * HEAD_DIM)
            pc = slice(p * 2 * CHUNK, (p + 1) * 2 * CHUNK)
            ks = r1s[u][:CHUNK].astype(BF16)
            rhs = jnp.concatenate([_pair_blocks(v_ref[b, rows, pw]), _pair_blocks(ks)], axis=0)
            lhs = jnp.concatenate([tb_s[b, c, :, pc], tbe_s[b, c, :, pc]], axis=1)
            vns.append(_pair_blocks(_dot(lhs, rhs).astype(BF16)))
        for u, (b, p) in enumerate(units):
            pw = slice(p * 2 * HEAD_DIM, (p + 1) * 2 * HEAD_DIM)
            kd = kdec_s[b, rows, pw]
            ds = _dot_tn(jnp.concatenate([kd[:, :HEAD_DIM], kd[:, HEAD_DIM:]], axis=0), vns[u])
            state[b, p] = state[b, p] * gtot_s[b, c][0:1, pw] + ds
        for u, (b, p) in enumerate(units):
            pw = slice(p * 2 * HEAD_DIM, (p + 1) * 2 * HEAD_DIM)
            pc = slice(p * 2 * CHUNK, (p + 1) * 2 * CHUNK)
            qs = r1s[u][CHUNK:].astype(BF16)
            rhs = jnp.concatenate([vns[u], _pair_blocks(qs)], axis=0)
            lhs = jnp.concatenate([qk_s[b, c, :, pc], deg_s[b, c, :, pc]], axis=1)
            o_ref[b, rows, pw] = _dot(lhs, rhs)


def _back_kernel(o_ref, a_ref, b_ref, *refs, from_tokens, final):
    h_refs, refs = _split_stream(refs, from_tokens)
    (wout_ref, gain_ref, wup_ref, cw_ref, wdown_ref, gfin_ref, out_ref, cg, cv) = refs
    i = pl.program_id(1)

    @pl.when(i == 0)
    def _():
        zeros = jnp.zeros((SUBLANES, D_FF), F32)
        cg[0:SUBLANES, :] = zeros
        cv[0:SUBLANES, :] = zeros

    ys = []
    for h in range(HEADS):
        hs = slice(h * HEAD_DIM, (h + 1) * HEAD_DIM)
        oh = o_ref[0, :, hs]
        ms = jnp.mean(oh * oh, axis=-1, keepdims=True)
        ys.append(oh * lax.rsqrt(ms + NORM_EPS) * a_ref[0, :, hs].astype(F32)
                  + b_ref[0, :, hs].astype(F32))
    y = jnp.concatenate(ys, axis=-1).astype(BF16)
    h1 = _stream_tile(i, h_refs) + _dot(y, wout_ref[...])

    un = _rms(h1, gain_ref[...]).astype(BF16)
    cw = cw_ref[...]
    slabs = [slice(lo, hi) for lo, hi in zip(FF_SPLITS[:-1], FF_SPLITS[1:])]

    def shifted(cs):
        return slice(D_FF + cs.start, D_FF + cs.stop)

    def up(cs):
        cg[SUBLANES:, cs] = _dot(un, wup_ref[:, cs])
        cv[SUBLANES:, cs] = _dot(un, wup_ref[:, shifted(cs)])

    for cs in slabs[:2]:
        up(cs)
    h2 = h1
    for s, cs in enumerate(slabs):
        half_gate = _causal_conv(cg, cs, cw[:, cs], FFN_CONV)
        val = _causal_conv(cv, cs, cw[:, shifted(cs)], FFN_CONV)
        act = (_silu_of_twice(half_gate) * val).astype(BF16)
        if s + 2 < len(slabs):
            up(slabs[s + 2])
        h2 = h2 + _dot(act, wdown_ref[cs, :])
    if final:
        h2 = _rms(h2, gfin_ref[...])
    out_ref[0] = h2


def _resident(shape):
    nd = len(shape)
    return pl.BlockSpec(shape, lambda b, i: (0,) * nd, pipeline_mode=pl.Buffered(1))


def _tile(width):
    return pl.BlockSpec((1, TM, width), lambda b, i: (b, i, 0))


def _params():
    return pltpu.CompilerParams(dimension_semantics=("arbitrary", "arbitrary"),
                                vmem_limit_bytes=VMEM_LIMIT)


def _token_tile():
    return pl.BlockSpec((1, TM, D_MODEL), lambda b, i: (b, jnp.maximum(i - 1, 0), 0))


def _stream_specs(stream):
    if len(stream) == 1:
        return [_tile(D_MODEL)]
    return [_token_tile(), _resident(stream[1].shape)]


def _layer_resident(stacked, layer):
    rest = stacked.shape[1:]
    return pl.BlockSpec((None,) + rest, lambda b, i: (layer,) + (0,) * len(rest),
                        pipeline_mode=pl.Buffered(1))


def _front(stream, t, layer, stacked, tri):
    bsz = stream[0].shape[0]
    nt = t // TM
    nc = TM // CHUNK
    act = jax.ShapeDtypeStruct((bsz, t, D_MODEL), BF16)
    return pl.pallas_call(
        functools.partial(_front_kernel, from_tokens=len(stream) == 2),
        grid=(bsz, nt),
        in_specs=_stream_specs(stream) + [_layer_resident(s, layer) for s in stacked]
        + [_resident(tri.shape)],
        out_specs=[_tile(D_MODEL)] * 5 + [
            _tile(LANES),
            pl.BlockSpec((1, nc, 2, CAT), lambda b, i: (b, i, 0, 0))],
        out_shape=[act] * 5 + [
            jax.ShapeDtypeStruct((bsz, t, LANES), F32),
            jax.ShapeDtypeStruct((bsz, t // CHUNK, 2, CAT), F32)],
        scratch_shapes=[pltpu.VMEM((SUBLANES + TM, D_MODEL), F32)] * 3
        + [pltpu.VMEM((POOL_CARRY, POOL_WIDTH), F32)]
        + [pltpu.VMEM((TM, D_MODEL), F32)] * 3
        + [pltpu.VMEM((TM, POOL_WIDTH + LANES), F32)],
        compiler_params=_params(),
        name="front",
    )(*stream, *stacked, tri)


def _delta(q, k, v, gcol, grow):
    bsz, t, _ = q.shape
    nt = t // TM
    nc = TM // CHUNK
    def both(width):
        return pl.BlockSpec((bsz, TM, width), lambda i: (0, i, 0))

    chunk_mats = pltpu.VMEM((bsz, nc, CHUNK, CAT), BF16)
    return pl.pallas_call(
        _delta_kernel,
        grid=(nt,),
        in_specs=[both(D_MODEL)] * 3 + [
            both(LANES),
            pl.BlockSpec((bsz, nc, 2, CAT), lambda i: (0, i, 0, 0))],
        out_specs=both(D_MODEL),
        out_shape=jax.ShapeDtypeStruct((bsz, t, D_MODEL), F32),
        scratch_shapes=[pltpu.VMEM((bsz, HEADS // 2, HEAD_DIM, 2 * HEAD_DIM), F32)]
        + [chunk_mats] * 4
        + [pltpu.VMEM((bsz, TM, D_MODEL), BF16),
           pltpu.VMEM((bsz, nc, SUBLANES, D_MODEL), F32)],
        compiler_params=pltpu.CompilerParams(dimension_semantics=("arbitrary",),
                                             vmem_limit_bytes=VMEM_LIMIT),
        name="delta",
    )(q, k, v, gcol, grow)


def _back(o, a, b, stream, layer, stacked, gfin, final):
    bsz, t, _ = o.shape
    nt = t // TM
    if final:
        out_spec, out_len = _token_tile(), t - TM
    else:
        out_spec, out_len = _tile(D_MODEL), t
    return pl.pallas_call(
        functools.partial(_back_kernel, from_tokens=len(stream) == 2, final=final),
        grid=(bsz, nt),
        in_specs=[_tile(D_MODEL)] * 3 + _stream_specs(stream)
        + [_layer_resident(s, layer) for s in stacked] + [_resident(gfin.shape)],
        out_specs=out_spec,
        out_shape=jax.ShapeDtypeStruct((bsz, out_len, D_MODEL), F32),
        scratch_shapes=[pltpu.VMEM((SUBLANES + TM, D_FF), F32)] * 2,
        compiler_params=_params(),
        name="back",
    )(o, a, b, *stream, *stacked, gfin)


def _block_tri():
    r = jnp.arange(TM)
    same_chunk = (r[:, None] // CHUNK) == (r[None, :] // CHUNK)
    return (same_chunk & (r[:, None] >= r[None, :])).astype(BF16)


def _lane_vecs(vals, offset):
    depth, n = vals.shape
    return jnp.zeros((depth, 1, LANES), F32).at[:, 0, offset:offset + n].set(vals.astype(F32))


def _pack_w_in(w_in):
    depth = w_in.shape[0]
    split = QKV_DIM + D_MODEL
    padded = jnp.concatenate(
        [w_in[..., :split], jnp.zeros((depth, D_MODEL, BA_PAD), w_in.dtype), w_in[..., split:]], axis=-1)
    scale = jnp.concatenate([jnp.ones((QKV_DIM,), F32), jnp.full((D_MODEL,), 0.5, F32),
                             jnp.ones((LANES + POOL_WIDTH,), F32), jnp.full((2 * D_MODEL,), 0.5, F32)])
    return (padded * scale).astype(BF16)


def kernel(x, meta_tokens, norm_mix, w_in, conv_qkv, a_log, dt_bias, head_norm, w_pool,
           pool_scale, w_out, norm_ffn, w_up, conv_ffn, w_down, norm_final):
    bsz, seq, _ = x.shape
    depth = w_in.shape[0]
    t = TM + seq
    first = jnp.concatenate([jnp.zeros((LEAD, D_MODEL), x.dtype), meta_tokens.astype(x.dtype)], axis=0)
    stream = (x, first)
    tri = _block_tri()
    front_params = (
        norm_mix[:, None, :], _pack_w_in(w_in), 0.5 * conv_qkv,
        _lane_vecs(a_log, DECAY_LANE), _lane_vecs(dt_bias, DECAY_LANE),
        0.5 * jnp.tile(head_norm, (1, HEADS))[:, None, :], w_pool.astype(BF16),
        0.5 * pool_scale[:, None, :])
    ffn_taps = jnp.concatenate([0.5 * conv_ffn[..., :D_FF], conv_ffn[..., D_FF:]], axis=-1)
    back_params = (w_out.astype(BF16), norm_ffn[:, None, :], w_up.astype(BF16), ffn_taps,
                   w_down.astype(BF16))
    for layer in range(depth):
        q, k, v, a, b, gcol, grow = _front(stream, t, layer, front_params, tri)
        o = _delta(q, k, v, gcol, grow)
        h = _back(o, a, b, stream, layer, back_params, norm_final[None, :], layer == depth - 1)
        stream = (h,)
    return h
```

```python
import functools

import jax
import jax.numpy as jnp
from jax import lax
from jax.experimental import pallas as pl
from jax.experimental.pallas import tpu as pltpu

D_MODEL = 1024
N_META = 16
HEADS = 8
HEAD_DIM = 128
QKV_DIM = 3 * D_MODEL
DN_CONV = 4
CHUNK = 64
POOL_WINDOWS = (2, 4, 8, 16)
POOL_GROUP_DIM = 128
POOL_WIDTH = 512
POOL_OUT_GROUP = 256
D_FF = 2816
FFN_CONV = 3
NORM_EPS = 1e-6

TM = 512
LEAD = TM - N_META
LANES = 128
SUBLANES = 8
POOL_CARRY = 16
GROUP = 4
CAT = HEADS * CHUNK
A_CHUNKS = 4
MXU_TILE = 256
PACK_ROWS = 256
FF_SPLITS = tuple(range(0, D_FF, 3 * MXU_TILE)) + (D_FF,)
VMEM_LIMIT = 56 * 1024 * 1024

BA_PAD = LANES - 2 * HEADS
BETA_LANE = BA_PAD
DECAY_LANE = BA_PAD + HEADS
_SECTIONS = (("q", D_MODEL), ("k", D_MODEL), ("v", D_MODEL), ("z", D_MODEL),
             ("ba_pool", LANES + POOL_WIDTH), ("gate_a", D_MODEL), ("gate_b", D_MODEL))
W_IN_COLS = {}
_at = 0
for _name, _width in _SECTIONS:
    W_IN_COLS[_name] = (_at, _at + _width)
    _at += _width

F32 = jnp.float32
BF16 = jnp.bfloat16


def _sigmoid(x):
    return 0.5 * jnp.tanh(0.5 * x) + 0.5


def _silu_of_twice(half):
    return half + half * jnp.tanh(half)


def _softplus(x):
    return jnp.maximum(x, 0.0) + jnp.log(1.0 + jnp.exp(-jnp.abs(x)))


def _dot(a, b):
    return jnp.dot(a, b, preferred_element_type=F32)


def _dot_nt(a, b):
    return lax.dot_general(a, b, (((1,), (1,)), ((), ())), preferred_element_type=F32)


def _dot_tn(a, b):
    return lax.dot_general(a, b, (((0,), (0,)), ((), ())), preferred_element_type=F32)


def _rms(x, gain):
    ms = jnp.mean(x * x, axis=-1, keepdims=True)
    return x * lax.rsqrt(ms + NORM_EPS) * gain


def _causal_conv(buf_ref, cs, cw, width):
    rows = buf_ref.shape[0] - SUBLANES
    groups = rows // SUBLANES
    pre = buf_ref[SUBLANES:, cs]
    cols = pre.shape[1]
    cur = pre.reshape(groups, SUBLANES, cols)
    prev = buf_ref[0:rows, cs].reshape(groups, SUBLANES, cols)
    sub = lax.broadcasted_iota(jnp.int32, (1, SUBLANES, cols), 1)
    acc = pre * cw[width - 1:width, :]
    for j in range(1, width):
        picked = jnp.where(sub < SUBLANES - j, cur, prev)
        shifted = pltpu.roll(picked, j, axis=1).reshape(rows, cols)
        acc = acc + shifted * cw[width - 1 - j:width - j, :]
    buf_ref[0:SUBLANES, cs] = pre[rows - SUBLANES:, :]
    return acc


def _split_stream(refs, from_tokens):
    n = 2 if from_tokens else 1
    return refs[:n], refs[n:]


def _stream_tile(i, h_refs):
    if len(h_refs) == 1:
        return h_refs[0][0]
    tok_ref, first_ref = h_refs
    return jnp.where(i == 0, first_ref[...], tok_ref[0])


def _front_kernel(*refs, from_tokens):
    h_refs, refs = _split_stream(refs, from_tokens)
    (gain_ref, w_ref, cw_ref, avec_ref, dtvec_ref, hn_ref, wpool_ref, ps_ref, tri_ref,
     q_out, k_out, v_out, a_out, b_out, gcol_out, grow_out,
     cq, ck, cv, cp, zbuf, gabuf, gbbuf, pbbuf) = refs
    i = pl.program_id(1)

    @pl.when(i == 0)
    def _():
        zeros = jnp.zeros((SUBLANES, D_MODEL), F32)
        cq[0:SUBLANES, :] = zeros
        ck[0:SUBLANES, :] = zeros
        cv[0:SUBLANES, :] = zeros
        cp[...] = jnp.zeros_like(cp)

    x = _stream_tile(i, h_refs)
    xn = _rms(x, gain_ref[...]).astype(BF16)
    cw = cw_ref[...]
    full = slice(0, D_MODEL)

    def proj(name):
        lo, hi = W_IN_COLS[name]
        return _dot(xn, w_ref[:, lo:hi])

    cq[SUBLANES:, :] = proj("q")
    ck[SUBLANES:, :] = proj("k")
    cv[SUBLANES:, :] = proj("v")
    gabuf[...] = proj("gate_a")
    zbuf[...] = proj("z")
    pbbuf[...] = proj("ba_pool")
    gbbuf[...] = proj("gate_b")

    def qkv_section(buf_ref, sec):
        return _silu_of_twice(
            _causal_conv(buf_ref, full, cw[:, sec * D_MODEL:(sec + 1) * D_MODEL], DN_CONV))

    def l2norm_store(xs, out_ref, scale):
        for h in range(HEADS):
            hs = slice(h * HEAD_DIM, (h + 1) * HEAD_DIM)
            xh = xs[:, hs]
            ss = jnp.sum(xh * xh, axis=-1, keepdims=True)
            out_ref[0, :, hs] = (xh * (lax.rsqrt(ss + NORM_EPS) * scale)).astype(BF16)

    l2norm_store(qkv_section(cq, 0), q_out, HEAD_DIM ** -0.5)
    l2norm_store(qkv_section(ck, 1), k_out, 1.0)
    v_out[0] = qkv_section(cv, 2).astype(BF16)
    a_out[0] = (hn_ref[...] * _silu_of_twice(zbuf[...]) * (1.0 + jnp.tanh(gabuf[...]))).astype(BF16)

    ba = pbbuf[:, :LANES]
    p = pbbuf[:, LANES:]
    row = lax.broadcasted_iota(jnp.int32, (TM, 1), 0)
    pos = i * TM + row - LEAD
    ys = []
    for gi, win in enumerate(POOL_WINDOWS):
        gs = slice(gi * POOL_GROUP_DIM, (gi + 1) * POOL_GROUP_DIM)
        pg = p[:, gs]
        s = jnp.concatenate([cp[:, gs], pg], axis=0)
        sh = 1
        while sh < win:
            s = s + pltpu.roll(s, sh, axis=0)
            sh *= 2
        cnt = jnp.clip(pos + 1, 1, win).astype(F32)
        pooled = s[POOL_CARRY:, :] / cnt - pg
        ys.append(_dot(pooled.astype(BF16), wpool_ref[gi]))
    cp[...] = p[TM - POOL_CARRY:, :]
    yb = jnp.concatenate(ys, axis=-1)
    b_out[0] = ((1.0 + jnp.tanh(gbbuf[...])) * yb * ps_ref[...]).astype(BF16)

    lane = lax.broadcasted_iota(jnp.int32, (1, LANES), 1)
    valid = pos >= 0
    beta = jnp.where(valid, _sigmoid(ba), 0.0)
    g = jnp.where(valid, -jnp.exp(avec_ref[...]) * _softplus(ba + dtvec_ref[...]), 0.0)
    g = jnp.where(lane >= DECAY_LANE, g, 0.0)
    g_hi = g.astype(BF16)
    g_lo = (g - g_hi.astype(F32)).astype(BF16)
    tri = tri_ref[...]
    gc = _dot(tri, g_hi) + _dot(tri, g_lo)
    col = jnp.where(lane < DECAY_LANE, beta, gc)
    gcol_out[0] = col
    rowform = col.T
    for c in range(TM // CHUNK):
        cs = slice(c * CHUNK, (c + 1) * CHUNK)
        grow_out[0, c] = jnp.concatenate(
            [jnp.concatenate([rowform[r0 + h:r0 + h + 1, cs] for h in range(HEADS)], axis=1)
             for r0 in (BETA_LANE, DECAY_LANE)], axis=0)


def _pair_blocks(x):
    a, b = x[:, :HEAD_DIM], x[:, HEAD_DIM:]
    z = jnp.zeros_like(a)
    return jnp.concatenate([jnp.concatenate([a, z], axis=1), jnp.concatenate([z, b], axis=1)], axis=0)


def _delta_kernel(q_ref, k_ref, v_ref, gcol_ref, grow_ref, o_ref,
                  state, tb_s, tbe_s, qk_s, deg_s, kdec_s, gtot_s):
    i = pl.program_id(0)
    nb = q_ref.shape[0]
    gw = GROUP * CHUNK

    @pl.when(i == 0)
    def _():
        state[...] = jnp.zeros_like(state)

    ri = lax.broadcasted_iota(jnp.int32, (CHUNK, gw), 0)
    ci = lax.broadcasted_iota(jnp.int32, (CHUNK, gw), 1) & (CHUNK - 1)
    causal = ri >= ci
    strict = ri > ci
    eye = ri == ci
    bd_mask = ((lax.broadcasted_iota(jnp.int32, (gw, gw), 0) >> 6)
               == (lax.broadcasted_iota(jnp.int32, (gw, gw), 1) >> 6))
    lane_half = lax.broadcasted_iota(jnp.int32, (CHUNK, LANES), 1) >> 6
    zblk = jnp.zeros((CHUNK, HEAD_DIM), BF16)

    def block_diag(pb):
        return jnp.where(bd_mask, jnp.concatenate([pb] * GROUP, axis=0), jnp.zeros((), BF16))

    def lane_bcast(col, idx):
        return jnp.take_along_axis(col, idx, axis=1)

    def phase_a(j, carry):
        xs, ps, meta = [], [], []
        for t in range(A_CHUNKS):
            c = j * A_CHUNKS + t
            rows = pl.ds(pl.multiple_of(c * CHUNK, CHUNK), CHUNK)
            for b in range(nb):
                col = gcol_ref[b, rows, :]
                rowc = grow_ref[b, c]
                beta_r, gc_r = rowc[0:1, :], rowc[1:2, :]
                beta_c = jnp.concatenate(
                    [lane_bcast(col, BETA_LANE + 2 * p + lane_half) for p in range(HEADS // 2)], axis=1)
                gc_c = jnp.concatenate(
                    [lane_bcast(col, DECAY_LANE + 2 * p + lane_half) for p in range(HEADS // 2)], axis=1)
                last = col[CHUNK - 1:CHUNK, :]
                ek_col = jnp.exp(last - col)
                gtot = jnp.exp(last)
                k_all = k_ref[b, rows, :]
                q_all = q_ref[b, rows, :]
                kdec, gts = [], []
                for h in range(HEADS):
                    hs = slice(h * HEAD_DIM, (h + 1) * HEAD_DIM)
                    ek_b = lane_bcast(ek_col, jnp.full((CHUNK, LANES), DECAY_LANE + h, jnp.int32))
                    kdec.append((k_all[:, hs].astype(F32) * ek_b).astype(BF16))
                    gts.append(jnp.broadcast_to(gtot[:, DECAY_LANE + h:DECAY_LANE + h + 1],
                                                (SUBLANES, HEAD_DIM)))
                kdec_s[b, rows, :] = jnp.concatenate(kdec, axis=1)
                gtot_s[b, c] = jnp.concatenate(gts, axis=1)
                for g in range(HEADS // GROUP):
                    gl = slice(g * GROUP * HEAD_DIM, (g + 1) * GROUP * HEAD_DIM)
                    cb = slice(g * gw, (g + 1) * gw)
                    kg = k_all[:, gl]
                    lhs = jnp.concatenate([kg, q_all[:, gl]], axis=0)
                    rhs_t = jnp.concatenate(
                        [jnp.concatenate([kg[:, m * HEAD_DIM:(m + 1) * HEAD_DIM] if m == h else zblk
                                          for m in range(GROUP)], axis=1) for h in range(GROUP)], axis=0)
                    sc = _dot_nt(lhs, rhs_t)
                    decay = jnp.exp(jnp.minimum(gc_c[:, cb] - gc_r[:, cb], 0.0))
                    p0 = jnp.where(strict, -(sc[:CHUNK] * beta_c[:, cb] * decay), 0.0)
                    qk_s[b, c, :, cb] = jnp.where(causal, sc[CHUNK:] * decay, 0.0).astype(BF16)
                    eg_r = jnp.exp(gc_r[:, cb])
                    deg_s[b, c, :, cb] = jnp.where(eye, eg_r, 0.0).astype(BF16)
                    xs.append(jnp.where(eye, 1.0, 0.0) + p0)
                    ps.append(p0)
                    meta.append((b, c, cb, beta_r[:, cb], eg_r))
        for s in range(6):
            for u in range(len(xs)):
                pb = ps[u].astype(BF16)
                bd = block_diag(pb)
                if s == 0:
                    ps[u] = _dot(pb, bd)
                elif s < 5:
                    r = _dot(jnp.concatenate([xs[u].astype(BF16), pb], axis=0), bd)
                    xs[u] = xs[u] + r[:CHUNK]
                    ps[u] = r[CHUNK:]
                else:
                    xs[u] = xs[u] + _dot(xs[u].astype(BF16), bd)
        for u, (b, c, cb, beta_r, eg_r) in enumerate(meta):
            tb = xs[u] * beta_r
            tb_s[b, c, :, cb] = tb.astype(BF16)
            tbe_s[b, c, :, cb] = (-(tb * eg_r)).astype(BF16)
        return carry

    lax.fori_loop(0, (TM // CHUNK) // A_CHUNKS, phase_a, 0)

    units = [(b, p) for b in range(nb) for p in range(HEADS // 2)]
    for c in range(TM // CHUNK):
        rows = slice(c * CHUNK, (c + 1) * CHUNK)
        r1s, vns = [], []
        for b, p in units:
            pw = slice(p * 2 * HEAD_DIM, (p + 1) * 2 * HEAD_DIM)
            kq = jnp.concatenate([k_ref[b, rows, pw], q_ref[b, rows, pw]], axis=0)
            r1s.append(_dot(kq, _pair_blocks(state[b, p].astype(BF16))))
        for u, (b, p) in enumerate(units):
            pw = slice(p * 2 * HEAD_DIM, (p + 1) * 2 * HEAD_DIM)
            pc = slice(p * 2 * CHUNK, (p + 1) * 2 * CHUNK)
            ks = r1s[u][:CHUNK].astype(BF16)
            rhs = jnp.concatenate([_pair_blocks(v_ref[b, rows, pw]), _pair_blocks(ks)], axis=0)
            lhs = jnp.concatenate([tb_s[b, c, :, pc], tbe_s[b, c, :, pc]], axis=1)
            vns.append(_pair_blocks(_dot(lhs, rhs).astype(BF16)))
        for u, (b, p) in enumerate(units):
            pw = slice(p * 2 * HEAD_DIM, (p + 1) * 2 * HEAD_DIM)
            kd = kdec_s[b, rows, pw]
            ds = _dot_tn(jnp.concatenate([kd[:, :HEAD_DIM], kd[:, HEAD_DIM:]], axis=0), vns[u])
            state[b, p] = state[b, p] * gtot_s[b, c][0:1, pw] + ds
        for u, (b, p) in enumerate(units):
            pw = slice(p * 2 * HEAD_DIM, (p + 1) * 2 * HEAD_DIM)
            pc = slice(p * 2 * CHUNK, (p + 1) * 2 * CHUNK)
            qs = r1s[u][CHUNK:].astype(BF16)
            rhs = jnp.concatenate([vns[u], _pair_blocks(qs)], axis=0)
            lhs = jnp.concatenate([qk_s[b, c, :, pc], deg_s[b, c, :, pc]], axis=1)
            o_ref[b, rows, pw] = _dot(lhs, rhs)


def _back_kernel(o_ref, a_ref, b_ref, *refs, from_tokens, final):
    h_refs, refs = _split_stream(refs, from_tokens)
    (wout_ref, gain_ref, wup_ref, cw_ref, wdown_ref, gfin_ref, out_ref, cg, cv) = refs
    i = pl.program_id(1)

    @pl.when(i == 0)
    def _():
        zeros = jnp.zeros((SUBLANES, D_FF), F32)
        cg[0:SUBLANES, :] = zeros
        cv[0:SUBLANES, :] = zeros

    ys = []
    for h in range(HEADS):
        hs = slice(h * HEAD_DIM, (h + 1) * HEAD_DIM)
        oh = o_ref[0, :, hs]
        ms = jnp.mean(oh * oh, axis=-1, keepdims=True)
        ys.append(oh * lax.rsqrt(ms + NORM_EPS) * a_ref[0, :, hs].astype(F32)
                  + b_ref[0, :, hs].astype(F32))
    y = jnp.concatenate(ys, axis=-1).astype(BF16)
    h1 = _stream_tile(i, h_refs) + _dot(y, wout_ref[...])

    un = _rms(h1, gain_ref[...]).astype(BF16)
    cw = cw_ref[...]
    slabs = [slice(lo, hi) for lo, hi in zip(FF_SPLITS[:-1], FF_SPLITS[1:])]

    def shifted(cs):
        return slice(D_FF + cs.start, D_FF + cs.stop)

    def up(cs):
        cg[SUBLANES:, cs] = _dot(un, wup_ref[:, cs])
        cv[SUBLANES:, cs] = _dot(un, wup_ref[:, shifted(cs)])

    for cs in slabs[:2]:
        up(cs)
    h2 = h1
    for s, cs in enumerate(slabs):
        half_gate = _causal_conv(cg, cs, cw[:, cs], FFN_CONV)
        val = _causal_conv(cv, cs, cw[:, shifted(cs)], FFN_CONV)
        act = (_silu_of_twice(half_gate) * val).astype(BF16)
        if s + 2 < len(slabs):
            up(slabs[s + 2])
        h2 = h2 + _dot(act, wdown_ref[cs, :])
    if final:
        h2 = _rms(h2, gfin_ref[...])
    out_ref[0] = h2


def _resident(shape):
    nd = len(shape)
    return pl.BlockSpec(shape, lambda b, i: (0,) * nd, pipeline_mode=pl.Buffered(1))


def _tile(width):
    return pl.BlockSpec((1, TM, width), lambda b, i: (b, i, 0))


def _params():
    return pltpu.CompilerParams(dimension_semantics=("arbitrary", "arbitrary"),
                                vmem_limit_bytes=VMEM_LIMIT)


def _token_tile():
    return pl.BlockSpec((1, TM, D_MODEL), lambda b, i: (b, jnp.maximum(i - 1, 0), 0))


def _stream_specs(stream):
    if len(stream) == 1:
        return [_tile(D_MODEL)]
    return [_token_tile(), _resident(stream[1].shape)]


def _layer_resident(stacked, layer):
    rest = stacked.shape[1:]
    return pl.BlockSpec((None,) + rest, lambda b, i: (layer,) + (0,) * len(rest),
                        pipeline_mode=pl.Buffered(1))


def _front(stream, t, layer, stacked, tri):
    bsz = stream[0].shape[0]
    nt = t // TM
    nc = TM // CHUNK
    act = jax.ShapeDtypeStruct((bsz, t, D_MODEL), BF16)
    return pl.pallas_call(
        functools.partial(_front_kernel, from_tokens=len(stream) == 2),
        grid=(bsz, nt),
        in_specs=_stream_specs(stream) + [_layer_resident(s, layer) for s in stacked]
        + [_resident(tri.shape)],
        out_specs=[_tile(D_MODEL)] * 5 + [
            _tile(LANES),
            pl.BlockSpec((1, nc, 2, CAT), lambda b, i: (b, i, 0, 0))],
        out_shape=[act] * 5 + [
            jax.ShapeDtypeStruct((bsz, t, LANES), F32),
            jax.ShapeDtypeStruct((bsz, t // CHUNK, 2, CAT), F32)],
        scratch_shapes=[pltpu.VMEM((SUBLANES + TM, D_MODEL), F32)] * 3
        + [pltpu.VMEM((POOL_CARRY, POOL_WIDTH), F32)]
        + [pltpu.VMEM((TM, D_MODEL), F32)] * 3
        + [pltpu.VMEM((TM, POOL_WIDTH + LANES), F32)],
        compiler_params=_params(),
        name="front",
    )(*stream, *stacked, tri)


def _delta(q, k, v, gcol, grow):
    bsz, t, _ = q.shape
    nt = t // TM
    nc = TM // CHUNK
    def both(width):
        return pl.BlockSpec((bsz, TM, width), lambda i: (0, i, 0))

    chunk_mats = pltpu.VMEM((bsz, nc, CHUNK, CAT), BF16)
    return pl.pallas_call(
        _delta_kernel,
        grid=(nt,),
        in_specs=[both(D_MODEL)] * 3 + [
            both(LANES),
            pl.BlockSpec((bsz, nc, 2, CAT), lambda i: (0, i, 0, 0))],
        out_specs=both(D_MODEL),
        out_shape=jax.ShapeDtypeStruct((bsz, t, D_MODEL), F32),
        scratch_shapes=[pltpu.VMEM((bsz, HEADS // 2, HEAD_DIM, 2 * HEAD_DIM), F32)]
        + [chunk_mats] * 4
        + [pltpu.VMEM((bsz, TM, D_MODEL), BF16),
           pltpu.VMEM((bsz, nc, SUBLANES, D_MODEL), F32)],
        compiler_params=pltpu.CompilerParams(dimension_semantics=("arbitrary",),
                                             vmem_limit_bytes=VMEM_LIMIT),
        name="delta",
    )(q, k, v, gcol, grow)


def _back(o, a, b, stream, layer, stacked, gfin, final):
    bsz, t, _ = o.shape
    nt = t // TM
    if final:
        out_spec, out_len = _token_tile(), t - TM
    else:
        out_spec, out_len = _tile(D_MODEL), t
    return pl.pallas_call(
        functools.partial(_back_kernel, from_tokens=len(stream) == 2, final=final),
        grid=(bsz, nt),
        in_specs=[_tile(D_MODEL)] * 3 + _stream_specs(stream)
        + [_layer_resident(s, layer) for s in stacked] + [_resident(gfin.shape)],
        out_specs=out_spec,
        out_shape=jax.ShapeDtypeStruct((bsz, out_len, D_MODEL), F32),
        scratch_shapes=[pltpu.VMEM((SUBLANES + TM, D_FF), F32)] * 2,
        compiler_params=_params(),
        name="back",
    )(o, a, b, *stream, *stacked, gfin)


def _block_tri():
    r = jnp.arange(TM)
    same_chunk = (r[:, None] // CHUNK) == (r[None, :] // CHUNK)
    return (same_chunk & (r[:, None] >= r[None, :])).astype(BF16)


def _lane_vecs(vals, offset):
    depth, n = vals.shape
    return jnp.zeros((depth, 1, LANES), F32).at[:, 0, offset:offset + n].set(vals.astype(F32))


def _pack_w_in(w_in):
    depth, d_in, in_dim = w_in.shape
    rows = PACK_ROWS
    return pl.pallas_call(
        _pack_w_in_kernel,
        grid=(depth, d_in // rows),
        in_specs=[pl.BlockSpec((None, rows, in_dim), lambda l, r: (l, r, 0))],
        out_specs=pl.BlockSpec((None, rows, in_dim + BA_PAD), lambda l, r: (l, r, 0)),
        out_shape=jax.ShapeDtypeStruct((depth, d_in, in_dim + BA_PAD), BF16),
        compiler_params=_params(),
        name="pack_w_in",
    )(w_in)


def _pack_w_in_kernel(w_ref, o_ref):
    split = QKV_DIM + D_MODEL
    ungated = LANES + POOL_WIDTH
    o_ref[:, :QKV_DIM] = w_ref[:, :QKV_DIM].astype(BF16)
    o_ref[:, QKV_DIM:split] = (0.5 * w_ref[:, QKV_DIM:split]).astype(BF16)
    tail = w_ref[:, split:]
    shifted = jnp.concatenate([jnp.zeros((tail.shape[0], BA_PAD), F32), tail], axis=1)
    o_ref[:, split:split + ungated] = shifted[:, :ungated].astype(BF16)
    o_ref[:, split + ungated:] = (0.5 * shifted[:, ungated:]).astype(BF16)


def kernel(x, meta_tokens, norm_mix, w_in, conv_qkv, a_log, dt_bias, head_norm, w_pool,
           pool_scale, w_out, norm_ffn, w_up, conv_ffn, w_down, norm_final):
    bsz, seq, _ = x.shape
    depth = w_in.shape[0]
    t = TM + seq
    first = jnp.concatenate([jnp.zeros((LEAD, D_MODEL), x.dtype), meta_tokens.astype(x.dtype)], axis=0)
    stream = (x, first)
    tri = _block_tri()
    front_params = (
        norm_mix[:, None, :], _pack_w_in(w_in), 0.5 * conv_qkv,
        _lane_vecs(a_log, DECAY_LANE), _lane_vecs(dt_bias, DECAY_LANE),
        0.5 * jnp.tile(head_norm, (1, HEADS))[:, None, :], w_pool.astype(BF16),
        0.5 * pool_scale[:, None, :])
    ffn_taps = jnp.concatenate([0.5 * conv_ffn[..., :D_FF], conv_ffn[..., D_FF:]], axis=-1)
    back_params = (w_out.astype(BF16), norm_ffn[:, None, :], w_up.astype(BF16), ffn_taps,
                   w_down.astype(BF16))
    for layer in range(depth):
        q, k, v, a, b, gcol, grow = _front(stream, t, layer, front_params, tri)
        o = _delta(q, k, v, gcol, grow)
        h = _back(o, a, b, stream, layer, back_params, norm_final[None, :], layer == depth - 1)
        stream = (h,)
    return h
```

```python
import functools

import jax
import jax.numpy as jnp
from jax import lax
from jax.experimental import pallas as pl
from jax.experimental.pallas import tpu as pltpu

D_MODEL = 1024
N_META = 16
HEADS = 8
HEAD_DIM = 128
QKV_DIM = 3 * D_MODEL
DN_CONV = 4
CHUNK = 64
POOL_WINDOWS = (2, 4, 8, 16)
POOL_GROUP_DIM = 128
POOL_WIDTH = 512
POOL_OUT_GROUP = 256
D_FF = 2816
FFN_CONV = 3
NORM_EPS = 1e-6

TM = 512
LEAD = TM - N_META
ZERO_ROWS = TM - CHUNK
LANES = 128
SUBLANES = 8
POOL_CARRY = 16
GROUP = 4
CAT = HEADS * CHUNK
A_CHUNKS = 4
MXU_TILE = 256
PACK_ROWS = 256
FF_SPLITS = tuple(range(0, D_FF, 3 * MXU_TILE)) + (D_FF,)
VMEM_LIMIT = 56 * 1024 * 1024

BA_PAD = LANES - 2 * HEADS
BETA_LANE = BA_PAD
DECAY_LANE = BA_PAD + HEADS
_SECTIONS = (("q", D_MODEL), ("k", D_MODEL), ("v", D_MODEL), ("z", D_MODEL),
             ("ba_pool", LANES + POOL_WIDTH), ("gate_a", D_MODEL), ("gate_b", D_MODEL))
W_IN_COLS = {}
_at = 0
for _name, _width in _SECTIONS:
    W_IN_COLS[_name] = (_at, _at + _width)
    _at += _width

F32 = jnp.float32
BF16 = jnp.bfloat16


def _sigmoid(x):
    return 0.5 * jnp.tanh(0.5 * x) + 0.5


def _silu_of_twice(half):
    return half + half * jnp.tanh(half)


def _softplus(x):
    return jnp.maximum(x, 0.0) + jnp.log(1.0 + jnp.exp(-jnp.abs(x)))


def _dot(a, b):
    return jnp.dot(a, b, preferred_element_type=F32)


def _dot_nt(a, b):
    return lax.dot_general(a, b, (((1,), (1,)), ((), ())), preferred_element_type=F32)


def _dot_tn(a, b):
    return lax.dot_general(a, b, (((0,), (0,)), ((), ())), preferred_element_type=F32)


def _rms(x, gain):
    ms = jnp.mean(x * x, axis=-1, keepdims=True)
    return x * lax.rsqrt(ms + NORM_EPS) * gain


def _causal_conv(buf_ref, rows, cs, cw, width):
    groups = rows // SUBLANES
    pre = buf_ref[SUBLANES:SUBLANES + rows, cs]
    cols = pre.shape[1]
    cur = pre.reshape(groups, SUBLANES, cols)
    prev = buf_ref[0:rows, cs].reshape(groups, SUBLANES, cols)
    sub = lax.broadcasted_iota(jnp.int32, (1, SUBLANES, cols), 1)
    acc = pre * cw[width - 1:width, :]
    for j in range(1, width):
        picked = jnp.where(sub < SUBLANES - j, cur, prev)
        shifted = pltpu.roll(picked, j, axis=1).reshape(rows, cols)
        acc = acc + shifted * cw[width - 1 - j:width - j, :]
    buf_ref[0:SUBLANES, cs] = pre[rows - SUBLANES:, :]
    return acc


def _split_stream(refs, from_tokens):
    n = 2 if from_tokens else 1
    return refs[:n], refs[n:]


def _stream_rows(h_refs, first_tile, rs):
    if len(h_refs) == 1:
        return h_refs[0][0, rs, :]
    tok_ref, first_ref = h_refs
    return first_ref[rs, :] if first_tile else tok_ref[0, rs, :]


def _first_tile_split(i, body, zero_fill):
    @pl.when(i == 0)
    def _():
        zero_fill(ZERO_ROWS)
        body(True, ZERO_ROWS, TM - ZERO_ROWS)

    @pl.when(i > 0)
    def _():
        body(False, 0, TM)


def _front_kernel(*refs, from_tokens):
    h_refs, refs = _split_stream(refs, from_tokens)
    (gain_ref, w_ref, cw_ref, avec_ref, dtvec_ref, hn_ref, wpool_ref, ps_ref, tri_ref,
     q_out, k_out, v_out, a_out, b_out, gcol_out, grow_out,
     cq, ck, cv, cp, zbuf, gabuf, gbbuf, pbbuf) = refs
    i = pl.program_id(1)

    def zero_fill(rows):
        zeros = jnp.zeros((SUBLANES, D_MODEL), F32)
        cq[0:SUBLANES, :] = zeros
        ck[0:SUBLANES, :] = zeros
        cv[0:SUBLANES, :] = zeros
        cp[...] = jnp.zeros_like(cp)
        for out in (q_out, k_out, v_out, a_out, b_out, gcol_out):
            out[0, 0:rows, :] = jnp.zeros((rows,) + out.shape[2:], out.dtype)
        nchunk = rows // CHUNK
        grow_out[0, 0:nchunk] = jnp.zeros((nchunk,) + grow_out.shape[2:], F32)

    def body(first_tile, r0, n):
        rs = slice(r0, r0 + n)
        data = slice(SUBLANES, SUBLANES + n)
        xn = _rms(_stream_rows(h_refs, first_tile, rs), gain_ref[...]).astype(BF16)
        cw = cw_ref[...]
        full = slice(0, D_MODEL)

        def proj(name):
            lo, hi = W_IN_COLS[name]
            return _dot(xn, w_ref[:, lo:hi])

        cq[data, :] = proj("q")
        ck[data, :] = proj("k")
        cv[data, :] = proj("v")
        gabuf[0:n, :] = proj("gate_a")
        zbuf[0:n, :] = proj("z")
        pbbuf[0:n, :] = proj("ba_pool")
        gbbuf[0:n, :] = proj("gate_b")

        def qkv_section(buf_ref, sec):
            return _silu_of_twice(
                _causal_conv(buf_ref, n, full, cw[:, sec * D_MODEL:(sec + 1) * D_MODEL], DN_CONV))

        def l2norm_store(xs, out_ref, scale):
            for h in range(HEADS):
                hs = slice(h * HEAD_DIM, (h + 1) * HEAD_DIM)
                xh = xs[:, hs]
                ss = jnp.sum(xh * xh, axis=-1, keepdims=True)
                out_ref[0, rs, hs] = (xh * (lax.rsqrt(ss + NORM_EPS) * scale)).astype(BF16)

        l2norm_store(qkv_section(cq, 0), q_out, HEAD_DIM ** -0.5)
        l2norm_store(qkv_section(ck, 1), k_out, 1.0)
        v_out[0, rs, :] = qkv_section(cv, 2).astype(BF16)
        a_out[0, rs, :] = (hn_ref[...] * _silu_of_twice(zbuf[0:n, :])
                           * (1.0 + jnp.tanh(gabuf[0:n, :]))).astype(BF16)

        ba = pbbuf[0:n, :LANES]
        p = pbbuf[0:n, LANES:]
        row = lax.broadcasted_iota(jnp.int32, (n, 1), 0)
        pos = i * TM + r0 + row - LEAD
        ys = []
        for gi, win in enumerate(POOL_WINDOWS):
            gs = slice(gi * POOL_GROUP_DIM, (gi + 1) * POOL_GROUP_DIM)
            pg = p[:, gs]
            s = jnp.concatenate([cp[:, gs], pg], axis=0)
            sh = 1
            while sh < win:
                s = s + pltpu.roll(s, sh, axis=0)
                sh *= 2
            cnt = jnp.clip(pos + 1, 1, win).astype(F32)
            pooled = s[POOL_CARRY:, :] / cnt - pg
            ys.append(_dot(pooled.astype(BF16), wpool_ref[gi]))
        cp[...] = p[n - POOL_CARRY:, :]
        yb = jnp.concatenate(ys, axis=-1)
        b_out[0, rs, :] = ((1.0 + jnp.tanh(gbbuf[0:n, :])) * yb * ps_ref[...]).astype(BF16)

        lane = lax.broadcasted_iota(jnp.int32, (1, LANES), 1)
        valid = pos >= 0
        beta = jnp.where(valid, _sigmoid(ba), 0.0)
        g = jnp.where(valid, -jnp.exp(avec_ref[...]) * _softplus(ba + dtvec_ref[...]), 0.0)
        g = jnp.where(lane >= DECAY_LANE, g, 0.0)
        g_hi = g.astype(BF16)
        g_lo = (g - g_hi.astype(F32)).astype(BF16)
        tri = tri_ref[0:n, 0:n]
        gc = _dot(tri, g_hi) + _dot(tri, g_lo)
        col = jnp.where(lane < DECAY_LANE, beta, gc)
        gcol_out[0, rs, :] = col
        rowform = col.T
        for c in range(n // CHUNK):
            cs = slice(c * CHUNK, (c + 1) * CHUNK)
            grow_out[0, r0 // CHUNK + c] = jnp.concatenate(
                [jnp.concatenate([rowform[l0 + h:l0 + h + 1, cs] for h in range(HEADS)], axis=1)
                 for l0 in (BETA_LANE, DECAY_LANE)], axis=0)

    _first_tile_split(i, body, zero_fill)


def _pair_blocks(x):
    a, b = x[:, :HEAD_DIM], x[:, HEAD_DIM:]
    z = jnp.zeros_like(a)
    return jnp.concatenate([jnp.concatenate([a, z], axis=1), jnp.concatenate([z, b], axis=1)], axis=0)


def _delta_kernel(q_ref, k_ref, v_ref, gcol_ref, grow_ref, o_ref,
                  state, tb_s, tbe_s, qk_s, deg_s, kdec_s, gtot_s):
    i = pl.program_id(0)
    nb = q_ref.shape[0]
    gw = GROUP * CHUNK

    @pl.when(i == 0)
    def _():
        state[...] = jnp.zeros_like(state)

    ri = lax.broadcasted_iota(jnp.int32, (CHUNK, gw), 0)
    ci = lax.broadcasted_iota(jnp.int32, (CHUNK, gw), 1) & (CHUNK - 1)
    causal = ri >= ci
    strict = ri > ci
    eye = ri == ci
    bd_mask = ((lax.broadcasted_iota(jnp.int32, (gw, gw), 0) >> 6)
               == (lax.broadcasted_iota(jnp.int32, (gw, gw), 1) >> 6))
    lane_half = lax.broadcasted_iota(jnp.int32, (CHUNK, LANES), 1) >> 6
    zblk = jnp.zeros((CHUNK, HEAD_DIM), BF16)

    def block_diag(pb):
        return jnp.where(bd_mask, jnp.concatenate([pb] * GROUP, axis=0), jnp.zeros((), BF16))

    def lane_bcast(col, idx):
        return jnp.take_along_axis(col, idx, axis=1)

    def phase_a(j, carry):
        xs, ps, meta = [], [], []
        for t in range(A_CHUNKS):
            c = j * A_CHUNKS + t
            rows = pl.ds(pl.multiple_of(c * CHUNK, CHUNK), CHUNK)
            for b in range(nb):
                col = gcol_ref[b, rows, :]
                rowc = grow_ref[b, c]
                beta_r, gc_r = rowc[0:1, :], rowc[1:2, :]
                beta_c = jnp.concatenate(
                    [lane_bcast(col, BETA_LANE + 2 * p + lane_half) for p in range(HEADS // 2)], axis=1)
                gc_c = jnp.concatenate(
                    [lane_bcast(col, DECAY_LANE + 2 * p + lane_half) for p in range(HEADS // 2)], axis=1)
                last = col[CHUNK - 1:CHUNK, :]
                ek_col = jnp.exp(last - col)
                gtot = jnp.exp(last)
                k_all = k_ref[b, rows, :]
                q_all = q_ref[b, rows, :]
                kdec, gts = [], []
                for h in range(HEADS):
                    hs = slice(h * HEAD_DIM, (h + 1) * HEAD_DIM)
                    ek_b = lane_bcast(ek_col, jnp.full((CHUNK, LANES), DECAY_LANE + h, jnp.int32))
                    kdec.append((k_all[:, hs].astype(F32) * ek_b).astype(BF16))
                    gts.append(jnp.broadcast_to(gtot[:, DECAY_LANE + h:DECAY_LANE + h + 1],
                                                (SUBLANES, HEAD_DIM)))
                kdec_s[b, rows, :] = jnp.concatenate(kdec, axis=1)
                gtot_s[b, c] = jnp.concatenate(gts, axis=1)
                for g in range(HEADS // GROUP):
                    gl = slice(g * GROUP * HEAD_DIM, (g + 1) * GROUP * HEAD_DIM)
                    cb = slice(g * gw, (g + 1) * gw)
                    kg = k_all[:, gl]
                    lhs = jnp.concatenate([kg, q_all[:, gl]], axis=0)
                    rhs_t = jnp.concatenate(
                        [jnp.concatenate([kg[:, m * HEAD_DIM:(m + 1) * HEAD_DIM] if m == h else zblk
                                          for m in range(GROUP)], axis=1) for h in range(GROUP)], axis=0)
                    sc = _dot_nt(lhs, rhs_t)
                    decay = jnp.exp(jnp.minimum(gc_c[:, cb] - gc_r[:, cb], 0.0))
                    p0 = jnp.where(strict, -(sc[:CHUNK] * beta_c[:, cb] * decay), 0.0)
                    qk_s[b, c, :, cb] = jnp.where(causal, sc[CHUNK:] * decay, 0.0).astype(BF16)
                    eg_r = jnp.exp(gc_r[:, cb])
                    deg_s[b, c, :, cb] = jnp.where(eye, eg_r, 0.0).astype(BF16)
                    xs.append(jnp.where(eye, 1.0, 0.0) + p0)
                    ps.append(p0)
                    meta.append((b, c, cb, beta_r[:, cb], eg_r))
        for s in range(6):
            for u in range(len(xs)):
                pb = ps[u].astype(BF16)
                bd = block_diag(pb)
                if s == 0:
                    ps[u] = _dot(pb, bd)
                elif s < 5:
                    r = _dot(jnp.concatenate([xs[u].astype(BF16), pb], axis=0), bd)
                    xs[u] = xs[u] + r[:CHUNK]
                    ps[u] = r[CHUNK:]
                else:
                    xs[u] = xs[u] + _dot(xs[u].astype(BF16), bd)
        for u, (b, c, cb, beta_r, eg_r) in enumerate(meta):
            tb = xs[u] * beta_r
            tb_s[b, c, :, cb] = tb.astype(BF16)
            tbe_s[b, c, :, cb] = (-(tb * eg_r)).astype(BF16)
        return carry

    lax.fori_loop(0, (TM // CHUNK) // A_CHUNKS, phase_a, 0)

    units = [(b, p) for b in range(nb) for p in range(HEADS // 2)]
    for c in range(TM // CHUNK):
        rows = slice(c * CHUNK, (c + 1) * CHUNK)
        r1s, vns = [], []
        for b, p in units:
            pw = slice(p * 2 * HEAD_DIM, (p + 1) * 2 * HEAD_DIM)
            kq = jnp.concatenate([k_ref[b, rows, pw], q_ref[b, rows, pw]], axis=0)
            r1s.append(_dot(kq, _pair_blocks(state[b, p].astype(BF16))))
        for u, (b, p) in enumerate(units):
            pw = slice(p * 2 * HEAD_DIM, (p + 1) * 2 * HEAD_DIM)
            pc = slice(p * 2 * CHUNK, (p + 1) * 2 * CHUNK)
            ks = r1s[u][:CHUNK].astype(BF16)
            rhs = jnp.concatenate([_pair_blocks(v_ref[b, rows, pw]), _pair_blocks(ks)], axis=0)
            lhs = jnp.concatenate([tb_s[b, c, :, pc], tbe_s[b, c, :, pc]], axis=1)
            vns.append(_pair_blocks(_dot(lhs, rhs).astype(BF16)))
        for u, (b, p) in enumerate(units):
            pw = slice(p * 2 * HEAD_DIM, (p + 1) * 2 * HEAD_DIM)
            kd = kdec_s[b, rows, pw]
            ds = _dot_tn(jnp.concatenate([kd[:, :HEAD_DIM], kd[:, HEAD_DIM:]], axis=0), vns[u])
            state[b, p] = state[b, p] * gtot_s[b, c][0:1, pw] + ds
        for u, (b, p) in enumerate(units):
            pw = slice(p * 2 * HEAD_DIM, (p + 1) * 2 * HEAD_DIM)
            pc = slice(p * 2 * CHUNK, (p + 1) * 2 * CHUNK)
            qs = r1s[u][CHUNK:].astype(BF16)
            rhs = jnp.concatenate([vns[u], _pair_blocks(qs)], axis=0)
            lhs = jnp.concatenate([qk_s[b, c, :, pc], deg_s[b, c, :, pc]], axis=1)
            o_ref[b, rows, pw] = _dot(lhs, rhs)


def _back_kernel(o_ref, a_ref, b_ref, *refs, from_tokens, final):
    h_refs, refs = _split_stream(refs, from_tokens)
    (wout_ref, gain_ref, wup_ref, cw_ref, wdown_ref, gfin_ref, out_ref, cg, cv) = refs
    i = pl.program_id(1)

    def zero_fill(rows):
        zeros = jnp.zeros((SUBLANES, D_FF), F32)
        cg[0:SUBLANES, :] = zeros
        cv[0:SUBLANES, :] = zeros
        out_ref[0, 0:rows, :] = jnp.zeros((rows, D_MODEL), out_ref.dtype)

    def body(first_tile, r0, n):
        rs = slice(r0, r0 + n)
        data = slice(SUBLANES, SUBLANES + n)
        ys = []
        for h in range(HEADS):
            hs = slice(h * HEAD_DIM, (h + 1) * HEAD_DIM)
            oh = o_ref[0, rs, hs]
            ms = jnp.mean(oh * oh, axis=-1, keepdims=True)
            ys.append(oh * lax.rsqrt(ms + NORM_EPS) * a_ref[0, rs, hs].astype(F32)
                      + b_ref[0, rs, hs].astype(F32))
        y = jnp.concatenate(ys, axis=-1).astype(BF16)
        h1 = _stream_rows(h_refs, first_tile, rs) + _dot(y, wout_ref[...])

        un = _rms(h1, gain_ref[...]).astype(BF16)
        cw = cw_ref[...]
        slabs = [slice(lo, hi) for lo, hi in zip(FF_SPLITS[:-1], FF_SPLITS[1:])]

        def shifted(cs):
            return slice(D_FF + cs.start, D_FF + cs.stop)

        def up(cs):
            cg[data, cs] = _dot(un, wup_ref[:, cs])
            cv[data, cs] = _dot(un, wup_ref[:, shifted(cs)])

        for cs in slabs[:2]:
            up(cs)
        h2 = h1
        for s, cs in enumerate(slabs):
            half_gate = _causal_conv(cg, n, cs, cw[:, cs], FFN_CONV)
            val = _causal_conv(cv, n, cs, cw[:, shifted(cs)], FFN_CONV)
            act = (_silu_of_twice(half_gate) * val).astype(BF16)
            if s + 2 < len(slabs):
                up(slabs[s + 2])
            h2 = h2 + _dot(act, wdown_ref[cs, :])
        if final:
            h2 = _rms(h2, gfin_ref[...])
        out_ref[0, rs, :] = h2

    _first_tile_split(i, body, zero_fill)


def _resident(shape):
    nd = len(shape)
    return pl.BlockSpec(shape, lambda b, i: (0,) * nd, pipeline_mode=pl.Buffered(1))


def _tile(width):
    return pl.BlockSpec((1, TM, width), lambda b, i: (b, i, 0))


def _params():
    return pltpu.CompilerParams(dimension_semantics=("arbitrary", "arbitrary"),
                                vmem_limit_bytes=VMEM_LIMIT)


def _token_tile():
    return pl.BlockSpec((1, TM, D_MODEL), lambda b, i: (b, jnp.maximum(i - 1, 0), 0))


def _stream_specs(stream):
    if len(stream) == 1:
        return [_tile(D_MODEL)]
    return [_token_tile(), _resident(stream[1].shape)]


def _layer_resident(stacked, layer):
    rest = stacked.shape[1:]
    return pl.BlockSpec((None,) + rest, lambda b, i: (layer,) + (0,) * len(rest),
                        pipeline_mode=pl.Buffered(1))


def _front(stream, t, layer, stacked, tri):
    bsz = stream[0].shape[0]
    nt = t // TM
    nc = TM // CHUNK
    act = jax.ShapeDtypeStruct((bsz, t, D_MODEL), BF16)
    return pl.pallas_call(
        functools.partial(_front_kernel, from_tokens=len(stream) == 2),
        grid=(bsz, nt),
        in_specs=_stream_specs(stream) + [_layer_resident(s, layer) for s in stacked]
        + [_resident(tri.shape)],
        out_specs=[_tile(D_MODEL)] * 5 + [
            _tile(LANES),
            pl.BlockSpec((1, nc, 2, CAT), lambda b, i: (b, i, 0, 0))],
        out_shape=[act] * 5 + [
            jax.ShapeDtypeStruct((bsz, t, LANES), F32),
            jax.ShapeDtypeStruct((bsz, t // CHUNK, 2, CAT), F32)],
        scratch_shapes=[pltpu.VMEM((SUBLANES + TM, D_MODEL), F32)] * 3
        + [pltpu.VMEM((POOL_CARRY, POOL_WIDTH), F32)]
        + [pltpu.VMEM((TM, D_MODEL), F32)] * 3
        + [pltpu.VMEM((TM, POOL_WIDTH + LANES), F32)],
        compiler_params=_params(),
        name="front",
    )(*stream, *stacked, tri)


def _delta(q, k, v, gcol, grow):
    bsz, t, _ = q.shape
    nt = t // TM
    nc = TM // CHUNK
    def both(width):
        return pl.BlockSpec((bsz, TM, width), lambda i: (0, i, 0))

    chunk_mats = pltpu.VMEM((bsz, nc, CHUNK, CAT), BF16)
    return pl.pallas_call(
        _delta_kernel,
        grid=(nt,),
        in_specs=[both(D_MODEL)] * 3 + [
            both(LANES),
            pl.BlockSpec((bsz, nc, 2, CAT), lambda i: (0, i, 0, 0))],
        out_specs=both(D_MODEL),
        out_shape=jax.ShapeDtypeStruct((bsz, t, D_MODEL), F32),
        scratch_shapes=[pltpu.VMEM((bsz, HEADS // 2, HEAD_DIM, 2 * HEAD_DIM), F32)]
        + [chunk_mats] * 4
        + [pltpu.VMEM((bsz, TM, D_MODEL), BF16),
           pltpu.VMEM((bsz, nc, SUBLANES, D_MODEL), F32)],
        compiler_params=pltpu.CompilerParams(dimension_semantics=("arbitrary",),
                                             vmem_limit_bytes=VMEM_LIMIT),
        name="delta",
    )(q, k, v, gcol, grow)


def _back(o, a, b, stream, layer, stacked, gfin, final):
    bsz, t, _ = o.shape
    nt = t // TM
    if final:
        out_spec, out_len = _token_tile(), t - TM
    else:
        out_spec, out_len = _tile(D_MODEL), t
    return pl.pallas_call(
        functools.partial(_back_kernel, from_tokens=len(stream) == 2, final=final),
        grid=(bsz, nt),
        in_specs=[_tile(D_MODEL)] * 3 + _stream_specs(stream)
        + [_layer_resident(s, layer) for s in stacked] + [_resident(gfin.shape)],
        out_specs=out_spec,
        out_shape=jax.ShapeDtypeStruct((bsz, out_len, D_MODEL), F32),
        scratch_shapes=[pltpu.VMEM((SUBLANES + TM, D_FF), F32)] * 2,
        compiler_params=_params(),
        name="back",
    )(o, a, b, *stream, *stacked, gfin)


def _block_tri():
    r = jnp.arange(TM)
    same_chunk = (r[:, None] // CHUNK) == (r[None, :] // CHUNK)
    return (same_chunk & (r[:, None] >= r[None, :])).astype(BF16)


def _lane_vecs(vals, offset):
    depth, n = vals.shape
    return jnp.zeros((depth, 1, LANES), F32).at[:, 0, offset:offset + n].set(vals.astype(F32))


def _pack_w_in(w_in):
    depth, d_in, in_dim = w_in.shape
    rows = PACK_ROWS
    return pl.pallas_call(
        _pack_w_in_kernel,
        grid=(depth, d_in // rows),
        in_specs=[pl.BlockSpec((None, rows, in_dim), lambda l, r: (l, r, 0))],
        out_specs=pl.BlockSpec((None, rows, in_dim + BA_PAD), lambda l, r: (l, r, 0)),
        out_shape=jax.ShapeDtypeStruct((depth, d_in, in_dim + BA_PAD), BF16),
        compiler_params=_params(),
        name="pack_w_in",
    )(w_in)


def _pack_w_in_kernel(w_ref, o_ref):
    split = QKV_DIM + D_MODEL
    ungated = LANES + POOL_WIDTH
    o_ref[:, :QKV_DIM] = w_ref[:, :QKV_DIM].astype(BF16)
    o_ref[:, QKV_DIM:split] = (0.5 * w_ref[:, QKV_DIM:split]).astype(BF16)
    tail = w_ref[:, split:]
    shifted = jnp.concatenate([jnp.zeros((tail.shape[0], BA_PAD), F32), tail], axis=1)
    o_ref[:, split:split + ungated] = shifted[:, :ungated].astype(BF16)
    o_ref[:, split + ungated:] = (0.5 * shifted[:, ungated:]).astype(BF16)


def kernel(x, meta_tokens, norm_mix, w_in, conv_qkv, a_log, dt_bias, head_norm, w_pool,
           pool_scale, w_out, norm_ffn, w_up, conv_ffn, w_down, norm_final):
    bsz, seq, _ = x.shape
    depth = w_in.shape[0]
    t = TM + seq
    first = jnp.concatenate([jnp.zeros((LEAD, D_MODEL), x.dtype), meta_tokens.astype(x.dtype)], axis=0)
    stream = (x, first)
    tri = _block_tri()
    front_params = (
        norm_mix[:, None, :], _pack_w_in(w_in), 0.5 * conv_qkv,
        _lane_vecs(a_log, DECAY_LANE), _lane_vecs(dt_bias, DECAY_LANE),
        0.5 * jnp.tile(head_norm, (1, HEADS))[:, None, :], w_pool.astype(BF16),
        0.5 * pool_scale[:, None, :])
    ffn_taps = jnp.concatenate([0.5 * conv_ffn[..., :D_FF], conv_ffn[..., D_FF:]], axis=-1)
    back_params = (w_out.astype(BF16), norm_ffn[:, None, :], w_up.astype(BF16), ffn_taps,
                   w_down.astype(BF16))
    for layer in range(depth):
        q, k, v, a, b, gcol, grow = _front(stream, t, layer, front_params, tri)
        o = _delta(q, k, v, gcol, grow)
        h = _back(o, a, b, stream, layer, back_params, norm_final[None, :], layer == depth - 1)
        stream = (h,)
    return h
```

```python
import functools

import jax
import jax.numpy as jnp
from jax import lax
from jax.experimental import pallas as pl
from jax.experimental.pallas import tpu as pltpu

D_MODEL = 1024
N_META = 16
HEADS = 8
HEAD_DIM = 128
QKV_DIM = 3 * D_MODEL
DN_CONV = 4
CHUNK = 64
POOL_WINDOWS = (2, 4, 8, 16)
POOL_GROUP_DIM = 128
POOL_WIDTH = 512
POOL_OUT_GROUP = 256
D_FF = 2816
FFN_CONV = 3
NORM_EPS = 1e-6

TM = 512
LEAD = TM - N_META
ZERO_ROWS = TM - CHUNK
LANES = 128
SUBLANES = 8
POOL_CARRY = 16
GROUP = 4
CAT = HEADS * CHUNK
MXU_TILE = 256
PACK_ROWS = 256
FF_SPLITS = tuple(range(0, D_FF, 3 * MXU_TILE)) + (D_FF,)
VMEM_LIMIT = 56 * 1024 * 1024

BA_PAD = LANES - 2 * HEADS
BETA_LANE = BA_PAD
DECAY_LANE = BA_PAD + HEADS
_SECTIONS = (("q", D_MODEL), ("k", D_MODEL), ("v", D_MODEL), ("z", D_MODEL),
             ("ba_pool", LANES + POOL_WIDTH), ("gate_a", D_MODEL), ("gate_b", D_MODEL))
W_IN_COLS = {}
_at = 0
for _name, _width in _SECTIONS:
    W_IN_COLS[_name] = (_at, _at + _width)
    _at += _width

F32 = jnp.float32
BF16 = jnp.bfloat16


def _sigmoid(x):
    return 0.5 * jnp.tanh(0.5 * x) + 0.5


def _silu_of_twice(half):
    return half + half * jnp.tanh(half)


def _softplus(x):
    return jnp.maximum(x, 0.0) + jnp.log(1.0 + jnp.exp(-jnp.abs(x)))


def _dot(a, b):
    return jnp.dot(a, b, preferred_element_type=F32)


def _dot_nt(a, b):
    return lax.dot_general(a, b, (((1,), (1,)), ((), ())), preferred_element_type=F32)


def _dot_tn(a, b):
    return lax.dot_general(a, b, (((0,), (0,)), ((), ())), preferred_element_type=F32)


def _rms(x, gain):
    ms = jnp.mean(x * x, axis=-1, keepdims=True)
    return x * lax.rsqrt(ms + NORM_EPS) * gain


def _causal_conv(buf_ref, rows, cs, cw, width):
    groups = rows // SUBLANES
    pre = buf_ref[SUBLANES:SUBLANES + rows, cs]
    cols = pre.shape[1]
    cur = pre.reshape(groups, SUBLANES, cols)
    prev = buf_ref[0:rows, cs].reshape(groups, SUBLANES, cols)
    sub = lax.broadcasted_iota(jnp.int32, (1, SUBLANES, cols), 1)
    acc = pre * cw[width - 1:width, :]
    for j in range(1, width):
        picked = jnp.where(sub < SUBLANES - j, cur, prev)
        shifted = pltpu.roll(picked, j, axis=1).reshape(rows, cols)
        acc = acc + shifted * cw[width - 1 - j:width - j, :]
    buf_ref[0:SUBLANES, cs] = pre[rows - SUBLANES:, :]
    return acc


def _split_stream(refs, from_tokens):
    n = 2 if from_tokens else 1
    return refs[:n], refs[n:]


def _stream_rows(h_refs, first_tile, rs):
    if len(h_refs) == 1:
        return h_refs[0][0, rs, :]
    tok_ref, first_ref = h_refs
    return first_ref[rs, :] if first_tile else tok_ref[0, rs, :]


def _first_tile_split(i, body, zero_fill):
    @pl.when(i == 0)
    def _():
        zero_fill(ZERO_ROWS)
        body(True, ZERO_ROWS, TM - ZERO_ROWS)

    @pl.when(i > 0)
    def _():
        body(False, 0, TM)


def _front_kernel(*refs, from_tokens):
    h_refs, refs = _split_stream(refs, from_tokens)
    (gain_ref, w_ref, cw_ref, avec_ref, dtvec_ref, hn_ref, wpool_ref, ps_ref, tri_ref,
     q_out, k_out, v_out, a_out, b_out, gcol_out, grow_out,
     cq, ck, cv, cp, zbuf, gabuf, gbbuf, pbbuf) = refs
    i = pl.program_id(1)

    def zero_fill(rows):
        zeros = jnp.zeros((SUBLANES, D_MODEL), F32)
        cq[0:SUBLANES, :] = zeros
        ck[0:SUBLANES, :] = zeros
        cv[0:SUBLANES, :] = zeros
        cp[...] = jnp.zeros_like(cp)
        for out in (q_out, k_out, v_out, a_out, b_out, gcol_out):
            out[0, 0:rows, :] = jnp.zeros((rows,) + out.shape[2:], out.dtype)
        nchunk = rows // CHUNK
        grow_out[0, 0:nchunk] = jnp.zeros((nchunk,) + grow_out.shape[2:], F32)

    def body(first_tile, r0, n):
        rs = slice(r0, r0 + n)
        data = slice(SUBLANES, SUBLANES + n)
        xn = _rms(_stream_rows(h_refs, first_tile, rs), gain_ref[...]).astype(BF16)
        cw = cw_ref[...]
        full = slice(0, D_MODEL)

        def proj(name):
            lo, hi = W_IN_COLS[name]
            return _dot(xn, w_ref[:, lo:hi])

        cq[data, :] = proj("q")
        ck[data, :] = proj("k")
        cv[data, :] = proj("v")
        gabuf[0:n, :] = proj("gate_a")
        zbuf[0:n, :] = proj("z")
        pbbuf[0:n, :] = proj("ba_pool")
        gbbuf[0:n, :] = proj("gate_b")

        def qkv_section(buf_ref, sec):
            return _silu_of_twice(
                _causal_conv(buf_ref, n, full, cw[:, sec * D_MODEL:(sec + 1) * D_MODEL], DN_CONV))

        def l2norm_store(xs, out_ref, scale):
            for h in range(HEADS):
                hs = slice(h * HEAD_DIM, (h + 1) * HEAD_DIM)
                xh = xs[:, hs]
                ss = jnp.sum(xh * xh, axis=-1, keepdims=True)
                out_ref[0, rs, hs] = (xh * (lax.rsqrt(ss + NORM_EPS) * scale)).astype(BF16)

        l2norm_store(qkv_section(cq, 0), q_out, HEAD_DIM ** -0.5)
        l2norm_store(qkv_section(ck, 1), k_out, 1.0)
        v_out[0, rs, :] = qkv_section(cv, 2).astype(BF16)
        a_out[0, rs, :] = (hn_ref[...] * _silu_of_twice(zbuf[0:n, :])
                           * (1.0 + jnp.tanh(gabuf[0:n, :]))).astype(BF16)

        ba = pbbuf[0:n, :LANES]
        p = pbbuf[0:n, LANES:]
        row = lax.broadcasted_iota(jnp.int32, (n, 1), 0)
        pos = i * TM + r0 + row - LEAD
        ys = []
        for gi, win in enumerate(POOL_WINDOWS):
            gs = slice(gi * POOL_GROUP_DIM, (gi + 1) * POOL_GROUP_DIM)
            pg = p[:, gs]
            s = jnp.concatenate([cp[:, gs], pg], axis=0)
            sh = 1
            while sh < win:
                s = s + pltpu.roll(s, sh, axis=0)
                sh *= 2
            cnt = jnp.clip(pos + 1, 1, win).astype(F32)
            pooled = s[POOL_CARRY:, :] / cnt - pg
            ys.append(_dot(pooled.astype(BF16), wpool_ref[gi]))
        cp[...] = p[n - POOL_CARRY:, :]
        yb = jnp.concatenate(ys, axis=-1)
        b_out[0, rs, :] = ((1.0 + jnp.tanh(gbbuf[0:n, :])) * yb * ps_ref[...]).astype(BF16)

        lane = lax.broadcasted_iota(jnp.int32, (1, LANES), 1)
        valid = pos >= 0
        beta = jnp.where(valid, _sigmoid(ba), 0.0)
        g = jnp.where(valid, -jnp.exp(avec_ref[...]) * _softplus(ba + dtvec_ref[...]), 0.0)
        g = jnp.where(lane >= DECAY_LANE, g, 0.0)
        g_hi = g.astype(BF16)
        g_lo = (g - g_hi.astype(F32)).astype(BF16)
        tri = tri_ref[0:n, 0:n]
        gc = _dot(tri, g_hi) + _dot(tri, g_lo)
        col = jnp.where(lane < DECAY_LANE, beta, gc)
        gcol_out[0, rs, :] = col
        rowform = col.T
        for c in range(n // CHUNK):
            cs = slice(c * CHUNK, (c + 1) * CHUNK)
            grow_out[0, r0 // CHUNK + c] = jnp.concatenate(
                [jnp.concatenate([rowform[l0 + h:l0 + h + 1, cs] for h in range(HEADS)], axis=1)
                 for l0 in (BETA_LANE, DECAY_LANE)], axis=0)

    _first_tile_split(i, body, zero_fill)


def _pair_blocks(x):
    a, b = x[:, :HEAD_DIM], x[:, HEAD_DIM:]
    z = jnp.zeros_like(a)
    return jnp.concatenate([jnp.concatenate([a, z], axis=1), jnp.concatenate([z, b], axis=1)], axis=0)


def _delta_kernel(q_ref, k_ref, v_ref, gcol_ref, grow_ref, o_ref,
                  state, tb_s, tbe_s, qk_s, deg_s, kdec_s, gtot_s):
    i = pl.program_id(0)
    nb = q_ref.shape[0]
    gw = GROUP * CHUNK

    ri = lax.broadcasted_iota(jnp.int32, (CHUNK, gw), 0)
    ci = lax.broadcasted_iota(jnp.int32, (CHUNK, gw), 1) & (CHUNK - 1)
    causal = ri >= ci
    strict = ri > ci
    eye = ri == ci
    bd_mask = ((lax.broadcasted_iota(jnp.int32, (gw, gw), 0) >> 6)
               == (lax.broadcasted_iota(jnp.int32, (gw, gw), 1) >> 6))
    lane_half = lax.broadcasted_iota(jnp.int32, (CHUNK, LANES), 1) >> 6
    zblk = jnp.zeros((CHUNK, HEAD_DIM), BF16)

    def block_diag(pb):
        return jnp.where(bd_mask, jnp.concatenate([pb] * GROUP, axis=0), jnp.zeros((), BF16))

    def lane_bcast(col, idx):
        return jnp.take_along_axis(col, idx, axis=1)

    def phase_a(chunks):
        xs, ps, meta = [], [], []
        for c in chunks:
            rows = slice(c * CHUNK, (c + 1) * CHUNK)
            for b in range(nb):
                col = gcol_ref[b, rows, :]
                rowc = grow_ref[b, c]
                beta_r, gc_r = rowc[0:1, :], rowc[1:2, :]
                beta_c = jnp.concatenate(
                    [lane_bcast(col, BETA_LANE + 2 * p + lane_half) for p in range(HEADS // 2)], axis=1)
                gc_c = jnp.concatenate(
                    [lane_bcast(col, DECAY_LANE + 2 * p + lane_half) for p in range(HEADS // 2)], axis=1)
                last = col[CHUNK - 1:CHUNK, :]
                ek_col = jnp.exp(last - col)
                gtot = jnp.exp(last)
                k_all = k_ref[b, rows, :]
                q_all = q_ref[b, rows, :]
                kdec, gts = [], []
                for h in range(HEADS):
                    hs = slice(h * HEAD_DIM, (h + 1) * HEAD_DIM)
                    ek_b = lane_bcast(ek_col, jnp.full((CHUNK, LANES), DECAY_LANE + h, jnp.int32))
                    kdec.append((k_all[:, hs].astype(F32) * ek_b).astype(BF16))
                    gts.append(jnp.broadcast_to(gtot[:, DECAY_LANE + h:DECAY_LANE + h + 1],
                                                (SUBLANES, HEAD_DIM)))
                kdec_s[b, rows, :] = jnp.concatenate(kdec, axis=1)
                gtot_s[b, c] = jnp.concatenate(gts, axis=1)
                for g in range(HEADS // GROUP):
                    gl = slice(g * GROUP * HEAD_DIM, (g + 1) * GROUP * HEAD_DIM)
                    cb = slice(g * gw, (g + 1) * gw)
                    kg = k_all[:, gl]
                    lhs = jnp.concatenate([kg, q_all[:, gl]], axis=0)
                    rhs_t = jnp.concatenate(
                        [jnp.concatenate([kg[:, m * HEAD_DIM:(m + 1) * HEAD_DIM] if m == h else zblk
                                          for m in range(GROUP)], axis=1) for h in range(GROUP)], axis=0)
                    sc = _dot_nt(lhs, rhs_t)
                    decay = jnp.exp(jnp.minimum(gc_c[:, cb] - gc_r[:, cb], 0.0))
                    p0 = jnp.where(strict, -(sc[:CHUNK] * beta_c[:, cb] * decay), 0.0)
                    qk_s[b, c, :, cb] = jnp.where(causal, sc[CHUNK:] * decay, 0.0).astype(BF16)
                    eg_r = jnp.exp(gc_r[:, cb])
                    deg_s[b, c, :, cb] = jnp.where(eye, eg_r, 0.0).astype(BF16)
                    xs.append(jnp.where(eye, 1.0, 0.0) + p0)
                    ps.append(p0)
                    meta.append((b, c, cb, beta_r[:, cb], eg_r))
        for s in range(6):
            for u in range(len(xs)):
                pb = ps[u].astype(BF16)
                bd = block_diag(pb)
                if s == 0:
                    ps[u] = _dot(pb, bd)
                elif s < 5:
                    r = _dot(jnp.concatenate([xs[u].astype(BF16), pb], axis=0), bd)
                    xs[u] = xs[u] + r[:CHUNK]
                    ps[u] = r[CHUNK:]
                else:
                    xs[u] = xs[u] + _dot(xs[u].astype(BF16), bd)
        for u, (b, c, cb, beta_r, eg_r) in enumerate(meta):
            tb = xs[u] * beta_r
            tb_s[b, c, :, cb] = tb.astype(BF16)
            tbe_s[b, c, :, cb] = (-(tb * eg_r)).astype(BF16)

    units = [(b, p) for b in range(nb) for p in range(HEADS // 2)]

    def phase_b(chunks):
        for c in chunks:
            rows = slice(c * CHUNK, (c + 1) * CHUNK)
            r1s, vns = [], []
            for b, p in units:
                pw = slice(p * 2 * HEAD_DIM, (p + 1) * 2 * HEAD_DIM)
                kq = jnp.concatenate([k_ref[b, rows, pw], q_ref[b, rows, pw]], axis=0)
                r1s.append(_dot(kq, _pair_blocks(state[b, p].astype(BF16))))
            for u, (b, p) in enumerate(units):
                pw = slice(p * 2 * HEAD_DIM, (p + 1) * 2 * HEAD_DIM)
                pc = slice(p * 2 * CHUNK, (p + 1) * 2 * CHUNK)
                ks = r1s[u][:CHUNK].astype(BF16)
                rhs = jnp.concatenate([_pair_blocks(v_ref[b, rows, pw]), _pair_blocks(ks)], axis=0)
                lhs = jnp.concatenate([tb_s[b, c, :, pc], tbe_s[b, c, :, pc]], axis=1)
                vns.append(_pair_blocks(_dot(lhs, rhs).astype(BF16)))
            for u, (b, p) in enumerate(units):
                pw = slice(p * 2 * HEAD_DIM, (p + 1) * 2 * HEAD_DIM)
                kd = kdec_s[b, rows, pw]
                ds = _dot_tn(jnp.concatenate([kd[:, :HEAD_DIM], kd[:, HEAD_DIM:]], axis=0), vns[u])
                state[b, p] = state[b, p] * gtot_s[b, c][0:1, pw] + ds
            for u, (b, p) in enumerate(units):
                pw = slice(p * 2 * HEAD_DIM, (p + 1) * 2 * HEAD_DIM)
                pc = slice(p * 2 * CHUNK, (p + 1) * 2 * CHUNK)
                qs = r1s[u][CHUNK:].astype(BF16)
                rhs = jnp.concatenate([vns[u], _pair_blocks(qs)], axis=0)
                lhs = jnp.concatenate([qk_s[b, c, :, pc], deg_s[b, c, :, pc]], axis=1)
                o_ref[b, rows, pw] = _dot(lhs, rhs)

    all_chunks = list(range(TM // CHUNK))

    @pl.when(i == 0)
    def _():
        state[...] = jnp.zeros_like(state)
        o_ref[:, 0:ZERO_ROWS, :] = jnp.zeros((nb, ZERO_ROWS, D_MODEL), o_ref.dtype)
        phase_a(all_chunks[-1:])
        phase_b(all_chunks[-1:])

    @pl.when(i > 0)
    def _():
        phase_a(all_chunks)
        phase_b(all_chunks)


def _back_kernel(o_ref, a_ref, b_ref, *refs, from_tokens, final):
    h_refs, refs = _split_stream(refs, from_tokens)
    (wout_ref, gain_ref, wup_ref, cw_ref, wdown_ref, gfin_ref, out_ref, cg, cv) = refs
    i = pl.program_id(1)

    def zero_fill(rows):
        zeros = jnp.zeros((SUBLANES, D_FF), F32)
        cg[0:SUBLANES, :] = zeros
        cv[0:SUBLANES, :] = zeros
        out_ref[0, 0:rows, :] = jnp.zeros((rows, D_MODEL), out_ref.dtype)

    def body(first_tile, r0, n):
        rs = slice(r0, r0 + n)
        data = slice(SUBLANES, SUBLANES + n)
        ys = []
        for h in range(HEADS):
            hs = slice(h * HEAD_DIM, (h + 1) * HEAD_DIM)
            oh = o_ref[0, rs, hs]
            ms = jnp.mean(oh * oh, axis=-1, keepdims=True)
            ys.append(oh * lax.rsqrt(ms + NORM_EPS) * a_ref[0, rs, hs].astype(F32)
                      + b_ref[0, rs, hs].astype(F32))
        y = jnp.concatenate(ys, axis=-1).astype(BF16)
        h1 = _stream_rows(h_refs, first_tile, rs) + _dot(y, wout_ref[...])

        un = _rms(h1, gain_ref[...]).astype(BF16)
        cw = cw_ref[...]
        slabs = [slice(lo, hi) for lo, hi in zip(FF_SPLITS[:-1], FF_SPLITS[1:])]

        def shifted(cs):
            return slice(D_FF + cs.start, D_FF + cs.stop)

        def up(cs):
            cg[data, cs] = _dot(un, wup_ref[:, cs])
            cv[data, cs] = _dot(un, wup_ref[:, shifted(cs)])

        for cs in slabs[:2]:
            up(cs)
        h2 = h1
        for s, cs in enumerate(slabs):
            half_gate = _causal_conv(cg, n, cs, cw[:, cs], FFN_CONV)
            val = _causal_conv(cv, n, cs, cw[:, shifted(cs)], FFN_CONV)
            act = (_silu_of_twice(half_gate) * val).astype(BF16)
            if s + 2 < len(slabs):
                up(slabs[s + 2])
            h2 = h2 + _dot(act, wdown_ref[cs, :])
        if final:
            h2 = _rms(h2, gfin_ref[...])
        out_ref[0, rs, :] = h2

    _first_tile_split(i, body, zero_fill)


def _resident(shape):
    nd = len(shape)
    return pl.BlockSpec(shape, lambda b, i: (0,) * nd, pipeline_mode=pl.Buffered(1))


def _tile(width):
    return pl.BlockSpec((1, TM, width), lambda b, i: (b, i, 0))


def _params():
    return pltpu.CompilerParams(dimension_semantics=("arbitrary", "arbitrary"),
                                vmem_limit_bytes=VMEM_LIMIT)


def _token_tile():
    return pl.BlockSpec((1, TM, D_MODEL), lambda b, i: (b, jnp.maximum(i - 1, 0), 0))


def _stream_specs(stream):
    if len(stream) == 1:
        return [_tile(D_MODEL)]
    return [_token_tile(), _resident(stream[1].shape)]


def _layer_resident(stacked, layer):
    rest = stacked.shape[1:]
    return pl.BlockSpec((None,) + rest, lambda b, i: (layer,) + (0,) * len(rest),
                        pipeline_mode=pl.Buffered(1))


def _front(stream, t, layer, stacked, tri):
    bsz = stream[0].shape[0]
    nt = t // TM
    nc = TM // CHUNK
    act = jax.ShapeDtypeStruct((bsz, t, D_MODEL), BF16)
    return pl.pallas_call(
        functools.partial(_front_kernel, from_tokens=len(stream) == 2),
        grid=(bsz, nt),
        in_specs=_stream_specs(stream) + [_layer_resident(s, layer) for s in stacked]
        + [_resident(tri.shape)],
        out_specs=[_tile(D_MODEL)] * 5 + [
            _tile(LANES),
            pl.BlockSpec((1, nc, 2, CAT), lambda b, i: (b, i, 0, 0))],
        out_shape=[act] * 5 + [
            jax.ShapeDtypeStruct((bsz, t, LANES), F32),
            jax.ShapeDtypeStruct((bsz, t // CHUNK, 2, CAT), F32)],
        scratch_shapes=[pltpu.VMEM((SUBLANES + TM, D_MODEL), F32)] * 3
        + [pltpu.VMEM((POOL_CARRY, POOL_WIDTH), F32)]
        + [pltpu.VMEM((TM, D_MODEL), F32)] * 3
        + [pltpu.VMEM((TM, POOL_WIDTH + LANES), F32)],
        compiler_params=_params(),
        name="front",
    )(*stream, *stacked, tri)


def _delta(q, k, v, gcol, grow):
    bsz, t, _ = q.shape
    nt = t // TM
    nc = TM // CHUNK
    def both(width):
        return pl.BlockSpec((bsz, TM, width), lambda i: (0, i, 0))

    chunk_mats = pltpu.VMEM((bsz, nc, CHUNK, CAT), BF16)
    return pl.pallas_call(
        _delta_kernel,
        grid=(nt,),
        in_specs=[both(D_MODEL)] * 3 + [
            both(LANES),
            pl.BlockSpec((bsz, nc, 2, CAT), lambda i: (0, i, 0, 0))],
        out_specs=both(D_MODEL),
        out_shape=jax.ShapeDtypeStruct((bsz, t, D_MODEL), F32),
        scratch_shapes=[pltpu.VMEM((bsz, HEADS // 2, HEAD_DIM, 2 * HEAD_DIM), F32)]
        + [chunk_mats] * 4
        + [pltpu.VMEM((bsz, TM, D_MODEL), BF16),
           pltpu.VMEM((bsz, nc, SUBLANES, D_MODEL), F32)],
        compiler_params=pltpu.CompilerParams(dimension_semantics=("arbitrary",),
                                             vmem_limit_bytes=VMEM_LIMIT),
        name="delta",
    )(q, k, v, gcol, grow)


def _back(o, a, b, stream, layer, stacked, gfin, final):
    bsz, t, _ = o.shape
    nt = t // TM
    if final:
        out_spec, out_len = _token_tile(), t - TM
    else:
        out_spec, out_len = _tile(D_MODEL), t
    return pl.pallas_call(
        functools.partial(_back_kernel, from_tokens=len(stream) == 2, final=final),
        grid=(bsz, nt),
        in_specs=[_tile(D_MODEL)] * 3 + _stream_specs(stream)
        + [_layer_resident(s, layer) for s in stacked] + [_resident(gfin.shape)],
        out_specs=out_spec,
        out_shape=jax.ShapeDtypeStruct((bsz, out_len, D_MODEL), F32),
        scratch_shapes=[pltpu.VMEM((SUBLANES + TM, D_FF), F32)] * 2,
        compiler_params=_params(),
        name="back",
    )(o, a, b, *stream, *stacked, gfin)


def _block_tri():
    r = jnp.arange(TM)
    same_chunk = (r[:, None] // CHUNK) == (r[None, :] // CHUNK)
    return (same_chunk & (r[:, None] >= r[None, :])).astype(BF16)


def _lane_vecs(vals, offset):
    depth, n = vals.shape
    return jnp.zeros((depth, 1, LANES), F32).at[:, 0, offset:offset + n].set(vals.astype(F32))


def _pack_w_in(w_in):
    depth, d_in, in_dim = w_in.shape
    rows = PACK_ROWS
    return pl.pallas_call(
        _pack_w_in_kernel,
        grid=(depth, d_in // rows),
        in_specs=[pl.BlockSpec((None, rows, in_dim), lambda l, r: (l, r, 0))],
        out_specs=pl.BlockSpec((None, rows, in_dim + BA_PAD), lambda l, r: (l, r, 0)),
        out_shape=jax.ShapeDtypeStruct((depth, d_in, in_dim + BA_PAD), BF16),
        compiler_params=_params(),
        name="pack_w_in",
    )(w_in)


def _pack_w_in_kernel(w_ref, o_ref):
    split = QKV_DIM + D_MODEL
    ungated = LANES + POOL_WIDTH
    o_ref[:, :QKV_DIM] = w_ref[:, :QKV_DIM].astype(BF16)
    o_ref[:, QKV_DIM:split] = (0.5 * w_ref[:, QKV_DIM:split]).astype(BF16)
    tail = w_ref[:, split:]
    shifted = jnp.concatenate([jnp.zeros((tail.shape[0], BA_PAD), F32), tail], axis=1)
    o_ref[:, split:split + ungated] = shifted[:, :ungated].astype(BF16)
    o_ref[:, split + ungated:] = (0.5 * shifted[:, ungated:]).astype(BF16)


def kernel(x, meta_tokens, norm_mix, w_in, conv_qkv, a_log, dt_bias, head_norm, w_pool,
           pool_scale, w_out, norm_ffn, w_up, conv_ffn, w_down, norm_final):
    bsz, seq, _ = x.shape
    depth = w_in.shape[0]
    t = TM + seq
    first = jnp.concatenate([jnp.zeros((LEAD, D_MODEL), x.dtype), meta_tokens.astype(x.dtype)], axis=0)
    stream = (x, first)
    tri = _block_tri()
    front_params = (
        norm_mix[:, None, :], _pack_w_in(w_in), 0.5 * conv_qkv,
        _lane_vecs(a_log, DECAY_LANE), _lane_vecs(dt_bias, DECAY_LANE),
        0.5 * jnp.tile(head_norm, (1, HEADS))[:, None, :], w_pool.astype(BF16),
        0.5 * pool_scale[:, None, :])
    ffn_taps = jnp.concatenate([0.5 * conv_ffn[..., :D_FF], conv_ffn[..., D_FF:]], axis=-1)
    back_params = (w_out.astype(BF16), norm_ffn[:, None, :], w_up.astype(BF16), ffn_taps,
                   w_down.astype(BF16))
    for layer in range(depth):
        q, k, v, a, b, gcol, grow = _front(stream, t, layer, front_params, tri)
        o = _delta(q, k, v, gcol, grow)
        h = _back(o, a, b, stream, layer, back_params, norm_final[None, :], layer == depth - 1)
        stream = (h,)
    return h
```

```python
import functools

import jax
import jax.numpy as jnp
from jax import lax
from jax.experimental import pallas as pl
from jax.experimental.pallas import tpu as pltpu

D_MODEL = 1024
N_META = 16
HEADS = 8
HEAD_DIM = 128
QKV_DIM = 3 * D_MODEL
DN_CONV = 4
CHUNK = 64
POOL_WINDOWS = (2, 4, 8, 16)
POOL_GROUP_DIM = 128
POOL_WIDTH = 512
POOL_OUT_GROUP = 256
D_FF = 2816
FFN_CONV = 3
NORM_EPS = 1e-6

TM = 512
LEAD = TM - N_META
ZERO_ROWS = TM - CHUNK
LANES = 128
SUBLANES = 8
POOL_CARRY = 16
GROUP = 4
CAT = HEADS * CHUNK
MXU_TILE = 256
PACK_ROWS = 256
FF_SPLITS = tuple(range(0, D_FF, 3 * MXU_TILE)) + (D_FF,)
VMEM_LIMIT = 56 * 1024 * 1024

BA_PAD = LANES - 2 * HEADS
BETA_LANE = BA_PAD
DECAY_LANE = BA_PAD + HEADS
_SECTIONS = (("q", D_MODEL), ("k", D_MODEL), ("v", D_MODEL), ("z", D_MODEL),
             ("ba_pool", LANES + POOL_WIDTH), ("gate_a", D_MODEL), ("gate_b", D_MODEL))
W_IN_COLS = {}
_at = 0
for _name, _width in _SECTIONS:
    W_IN_COLS[_name] = (_at, _at + _width)
    _at += _width

F32 = jnp.float32
BF16 = jnp.bfloat16


def _sigmoid(x):
    return 0.5 * jnp.tanh(0.5 * x) + 0.5


def _silu_of_twice(half):
    return half + half * jnp.tanh(half)


def _softplus(x):
    return jnp.maximum(x, 0.0) + jnp.log(1.0 + jnp.exp(-jnp.abs(x)))


def _dot(a, b):
    return jnp.dot(a, b, preferred_element_type=F32)


def _dot_nt(a, b):
    return lax.dot_general(a, b, (((1,), (1,)), ((), ())), preferred_element_type=F32)


def _dot_tn(a, b):
    return lax.dot_general(a, b, (((0,), (0,)), ((), ())), preferred_element_type=F32)


def _rms(x, gain):
    ms = jnp.mean(x * x, axis=-1, keepdims=True)
    return x * lax.rsqrt(ms + NORM_EPS) * gain


def _causal_conv(buf_ref, rows, cs, cw, width):
    groups = rows // SUBLANES
    pre = buf_ref[SUBLANES:SUBLANES + rows, cs]
    cols = pre.shape[1]
    cur = pre.reshape(groups, SUBLANES, cols)
    prev = buf_ref[0:rows, cs].reshape(groups, SUBLANES, cols)
    sub = lax.broadcasted_iota(jnp.int32, (1, SUBLANES, cols), 1)
    acc = pre * cw[width - 1:width, :]
    for j in range(1, width):
        picked = jnp.where(sub < SUBLANES - j, cur, prev)
        shifted = pltpu.roll(picked, j, axis=1).reshape(rows, cols)
        acc = acc + shifted * cw[width - 1 - j:width - j, :]
    buf_ref[0:SUBLANES, cs] = pre[rows - SUBLANES:, :]
    return acc


def _split_stream(refs, from_tokens):
    n = 2 if from_tokens else 1
    return refs[:n], refs[n:]


def _stream_rows(h_refs, first_tile, rs):
    if len(h_refs) == 1:
        return h_refs[0][0, rs, :]
    tok_ref, first_ref = h_refs
    return first_ref[rs, :] if first_tile else tok_ref[0, rs, :]


def _first_tile_split(i, body, zero_fill):
    @pl.when(i == 0)
    def _():
        zero_fill(ZERO_ROWS)
        body(True, ZERO_ROWS, TM - ZERO_ROWS)

    @pl.when(i > 0)
    def _():
        body(False, 0, TM)


def _front_kernel(*refs, from_tokens):
    h_refs, refs = _split_stream(refs, from_tokens)
    (gain_ref, w_ref, cw_ref, avec_ref, dtvec_ref, hn_ref, wpool_ref, ps_ref, tri_ref,
     q_out, k_out, v_out, a_out, b_out, gcol_out, grow_out,
     cq, ck, cv, cp, zbuf, gabuf, gbbuf, pbbuf) = refs
    i = pl.program_id(1)

    def zero_fill(rows):
        zeros = jnp.zeros((SUBLANES, D_MODEL), F32)
        cq[0:SUBLANES, :] = zeros
        ck[0:SUBLANES, :] = zeros
        cv[0:SUBLANES, :] = zeros
        cp[...] = jnp.zeros_like(cp)
        for out in (q_out, k_out, v_out, a_out, b_out, gcol_out):
            out[0, 0:rows, :] = jnp.zeros((rows,) + out.shape[2:], out.dtype)
        nchunk = rows // CHUNK
        grow_out[0, 0:nchunk] = jnp.zeros((nchunk,) + grow_out.shape[2:], F32)

    def body(first_tile, r0, n):
        rs = slice(r0, r0 + n)
        data = slice(SUBLANES, SUBLANES + n)
        xn = _rms(_stream_rows(h_refs, first_tile, rs), gain_ref[...]).astype(BF16)
        cw = cw_ref[...]
        full = slice(0, D_MODEL)

        def proj(name):
            lo, hi = W_IN_COLS[name]
            return _dot(xn, w_ref[:, lo:hi])

        cq[data, :] = proj("q")
        ck[data, :] = proj("k")
        cv[data, :] = proj("v")
        gabuf[0:n, :] = proj("gate_a")
        zbuf[0:n, :] = proj("z")
        pbbuf[0:n, :] = proj("ba_pool")
        gbbuf[0:n, :] = proj("gate_b")

        def qkv_section(buf_ref, sec):
            return _silu_of_twice(
                _causal_conv(buf_ref, n, full, cw[:, sec * D_MODEL:(sec + 1) * D_MODEL], DN_CONV))

        def l2norm_store(xs, out_ref, scale):
            for h in range(HEADS):
                hs = slice(h * HEAD_DIM, (h + 1) * HEAD_DIM)
                xh = xs[:, hs]
                ss = jnp.sum(xh * xh, axis=-1, keepdims=True)
                out_ref[0, rs, hs] = (xh * (lax.rsqrt(ss + NORM_EPS) * scale)).astype(BF16)

        l2norm_store(qkv_section(cq, 0), q_out, HEAD_DIM ** -0.5)
        l2norm_store(qkv_section(ck, 1), k_out, 1.0)
        v_out[0, rs, :] = qkv_section(cv, 2).astype(BF16)
        a_out[0, rs, :] = (hn_ref[...] * _silu_of_twice(zbuf[0:n, :])
                           * (1.0 + jnp.tanh(gabuf[0:n, :]))).astype(BF16)

        ba = pbbuf[0:n, :LANES]
        p = pbbuf[0:n, LANES:]
        row = lax.broadcasted_iota(jnp.int32, (n, 1), 0)
        pos = i * TM + r0 + row - LEAD
        ys = []
        for gi, win in enumerate(POOL_WINDOWS):
            gs = slice(gi * POOL_GROUP_DIM, (gi + 1) * POOL_GROUP_DIM)
            pg = p[:, gs]
            s = jnp.concatenate([cp[:, gs], pg], axis=0)
            sh = 1
            while sh < win:
                s = s + pltpu.roll(s, sh, axis=0)
                sh *= 2
            if first_tile:
                pooled = s[POOL_CARRY:, :] / jnp.clip(pos + 1, 1, win).astype(F32) - pg
            else:
                pooled = s[POOL_CARRY:, :] * (1.0 / win) - pg
            ys.append(_dot(pooled.astype(BF16), wpool_ref[gi]))
        cp[...] = p[n - POOL_CARRY:, :]
        yb = jnp.concatenate(ys, axis=-1)
        b_out[0, rs, :] = ((1.0 + jnp.tanh(gbbuf[0:n, :])) * yb * ps_ref[...]).astype(BF16)

        lane = lax.broadcasted_iota(jnp.int32, (1, LANES), 1)
        beta = _sigmoid(ba)
        g = -jnp.exp(avec_ref[...]) * _softplus(ba + dtvec_ref[...])
        if first_tile:
            valid = pos >= 0
            beta = jnp.where(valid, beta, 0.0)
            g = jnp.where(valid, g, 0.0)
        g = jnp.where(lane >= DECAY_LANE, g, 0.0)
        g_hi = g.astype(BF16)
        g_lo = (g - g_hi.astype(F32)).astype(BF16)
        tri = tri_ref[0:n, 0:n]
        gc = _dot(tri, g_hi) + _dot(tri, g_lo)
        col = jnp.where(lane < DECAY_LANE, beta, gc)
        gcol_out[0, rs, :] = col
        rowform = col.T
        for c in range(n // CHUNK):
            cs = slice(c * CHUNK, (c + 1) * CHUNK)
            grow_out[0, r0 // CHUNK + c] = jnp.concatenate(
                [jnp.concatenate([rowform[l0 + h:l0 + h + 1, cs] for h in range(HEADS)], axis=1)
                 for l0 in (BETA_LANE, DECAY_LANE)], axis=0)

    _first_tile_split(i, body, zero_fill)


def _pair_blocks(x):
    a, b = x[:, :HEAD_DIM], x[:, HEAD_DIM:]
    z = jnp.zeros_like(a)
    return jnp.concatenate([jnp.concatenate([a, z], axis=1), jnp.concatenate([z, b], axis=1)], axis=0)


def _delta_kernel(q_ref, k_ref, v_ref, gcol_ref, grow_ref, o_ref,
                  state, tb_s, tbe_s, qk_s, deg_s, kdec_s, gtot_s):
    i = pl.program_id(0)
    nb = q_ref.shape[0]
    gw = GROUP * CHUNK

    ri = lax.broadcasted_iota(jnp.int32, (CHUNK, gw), 0)
    ci = lax.broadcasted_iota(jnp.int32, (CHUNK, gw), 1) & (CHUNK - 1)
    causal = ri >= ci
    strict = ri > ci
    eye = ri == ci
    bd_mask = ((lax.broadcasted_iota(jnp.int32, (gw, gw), 0) >> 6)
               == (lax.broadcasted_iota(jnp.int32, (gw, gw), 1) >> 6))
    lane_half = lax.broadcasted_iota(jnp.int32, (CHUNK, LANES), 1) >> 6
    zblk = jnp.zeros((CHUNK, HEAD_DIM), BF16)

    def block_diag(pb):
        return jnp.where(bd_mask, jnp.concatenate([pb] * GROUP, axis=0), jnp.zeros((), BF16))

    def lane_bcast(col, idx):
        return jnp.take_along_axis(col, idx, axis=1)

    def phase_a(chunks):
        xs, ps, meta = [], [], []
        for c in chunks:
            rows = slice(c * CHUNK, (c + 1) * CHUNK)
            for b in range(nb):
                col = gcol_ref[b, rows, :]
                rowc = grow_ref[b, c]
                beta_r, gc_r = rowc[0:1, :], rowc[1:2, :]
                beta_c = jnp.concatenate(
                    [lane_bcast(col, BETA_LANE + 2 * p + lane_half) for p in range(HEADS // 2)], axis=1)
                gc_c = jnp.concatenate(
                    [lane_bcast(col, DECAY_LANE + 2 * p + lane_half) for p in range(HEADS // 2)], axis=1)
                last = col[CHUNK - 1:CHUNK, :]
                ek_col = jnp.exp(last - col)
                gtot = jnp.exp(last)
                k_all = k_ref[b, rows, :]
                q_all = q_ref[b, rows, :]
                kdec, gts = [], []
                for h in range(HEADS):
                    hs = slice(h * HEAD_DIM, (h + 1) * HEAD_DIM)
                    ek_b = lane_bcast(ek_col, jnp.full((CHUNK, LANES), DECAY_LANE + h, jnp.int32))
                    kdec.append((k_all[:, hs].astype(F32) * ek_b).astype(BF16))
                    gts.append(jnp.broadcast_to(gtot[:, DECAY_LANE + h:DECAY_LANE + h + 1],
                                                (SUBLANES, HEAD_DIM)))
                kdec_s[b, rows, :] = jnp.concatenate(kdec, axis=1)
                gtot_s[b, c] = jnp.concatenate(gts, axis=1)
                for g in range(HEADS // GROUP):
                    gl = slice(g * GROUP * HEAD_DIM, (g + 1) * GROUP * HEAD_DIM)
                    cb = slice(g * gw, (g + 1) * gw)
                    kg = k_all[:, gl]
                    lhs = jnp.concatenate([kg, q_all[:, gl]], axis=0)
                    rhs_t = jnp.concatenate(
                        [jnp.concatenate([kg[:, m * HEAD_DIM:(m + 1) * HEAD_DIM] if m == h else zblk
                                          for m in range(GROUP)], axis=1) for h in range(GROUP)], axis=0)
                    sc = _dot_nt(lhs, rhs_t)
                    decay = jnp.exp(jnp.minimum(gc_c[:, cb] - gc_r[:, cb], 0.0))
                    p0 = jnp.where(strict, -(sc[:CHUNK] * beta_c[:, cb] * decay), 0.0)
                    qk_s[b, c, :, cb] = jnp.where(causal, sc[CHUNK:] * decay, 0.0).astype(BF16)
                    eg_r = jnp.exp(gc_r[:, cb])
                    deg_s[b, c, :, cb] = jnp.where(eye, eg_r, 0.0).astype(BF16)
                    xs.append(jnp.where(eye, 1.0, 0.0) + p0)
                    ps.append(p0)
                    meta.append((b, c, cb, beta_r[:, cb], eg_r))
        for s in range(6):
            for u in range(len(xs)):
                pb = ps[u].astype(BF16)
                bd = block_diag(pb)
                if s == 0:
                    ps[u] = _dot(pb, bd)
                elif s < 5:
                    r = _dot(jnp.concatenate([xs[u].astype(BF16), pb], axis=0), bd)
                    xs[u] = xs[u] + r[:CHUNK]
                    ps[u] = r[CHUNK:]
                else:
                    xs[u] = xs[u] + _dot(xs[u].astype(BF16), bd)
        for u, (b, c, cb, beta_r, eg_r) in enumerate(meta):
            tb = xs[u] * beta_r
            tb_s[b, c, :, cb] = tb.astype(BF16)
            tbe_s[b, c, :, cb] = (-(tb * eg_r)).astype(BF16)

    units = [(b, p) for b in range(nb) for p in range(HEADS // 2)]

    def phase_b(chunks):
        for c in chunks:
            rows = slice(c * CHUNK, (c + 1) * CHUNK)
            r1s, vns = [], []
            for b, p in units:
                pw = slice(p * 2 * HEAD_DIM, (p + 1) * 2 * HEAD_DIM)
                kq = jnp.concatenate([k_ref[b, rows, pw], q_ref[b, rows, pw]], axis=0)
                r1s.append(_dot(kq, _pair_blocks(state[b, p].astype(BF16))))
            for u, (b, p) in enumerate(units):
                pw = slice(p * 2 * HEAD_DIM, (p + 1) * 2 * HEAD_DIM)
                pc = slice(p * 2 * CHUNK, (p + 1) * 2 * CHUNK)
                ks = r1s[u][:CHUNK].astype(BF16)
                rhs = jnp.concatenate([_pair_blocks(v_ref[b, rows, pw]), _pair_blocks(ks)], axis=0)
                lhs = jnp.concatenate([tb_s[b, c, :, pc], tbe_s[b, c, :, pc]], axis=1)
                vns.append(_pair_blocks(_dot(lhs, rhs).astype(BF16)))
            for u, (b, p) in enumerate(units):
                pw = slice(p * 2 * HEAD_DIM, (p + 1) * 2 * HEAD_DIM)
                kd = kdec_s[b, rows, pw]
                ds = _dot_tn(jnp.concatenate([kd[:, :HEAD_DIM], kd[:, HEAD_DIM:]], axis=0), vns[u])
                state[b, p] = state[b, p] * gtot_s[b, c][0:1, pw] + ds
            for u, (b, p) in enumerate(units):
                pw = slice(p * 2 * HEAD_DIM, (p + 1) * 2 * HEAD_DIM)
                pc = slice(p * 2 * CHUNK, (p + 1) * 2 * CHUNK)
                qs = r1s[u][CHUNK:].astype(BF16)
                rhs = jnp.concatenate([vns[u], _pair_blocks(qs)], axis=0)
                lhs = jnp.concatenate([qk_s[b, c, :, pc], deg_s[b, c, :, pc]], axis=1)
                o_ref[b, rows, pw] = _dot(lhs, rhs)

    all_chunks = list(range(TM // CHUNK))

    @pl.when(i == 0)
    def _():
        state[...] = jnp.zeros_like(state)
        o_ref[:, 0:ZERO_ROWS, :] = jnp.zeros((nb, ZERO_ROWS, D_MODEL), o_ref.dtype)
        phase_a(all_chunks[-1:])
        phase_b(all_chunks[-1:])

    @pl.when(i > 0)
    def _():
        phase_a(all_chunks)
        phase_b(all_chunks)


def _back_kernel(o_ref, a_ref, b_ref, *refs, from_tokens, final):
    h_refs, refs = _split_stream(refs, from_tokens)
    (wout_ref, gain_ref, wup_ref, cw_ref, wdown_ref, gfin_ref, out_ref, cg, cv) = refs
    i = pl.program_id(1)

    def zero_fill(rows):
        zeros = jnp.zeros((SUBLANES, D_FF), F32)
        cg[0:SUBLANES, :] = zeros
        cv[0:SUBLANES, :] = zeros
        out_ref[0, 0:rows, :] = jnp.zeros((rows, D_MODEL), out_ref.dtype)

    def body(first_tile, r0, n):
        rs = slice(r0, r0 + n)
        data = slice(SUBLANES, SUBLANES + n)
        ys = []
        for h in range(HEADS):
            hs = slice(h * HEAD_DIM, (h + 1) * HEAD_DIM)
            oh = o_ref[0, rs, hs]
            ms = jnp.mean(oh * oh, axis=-1, keepdims=True)
            ys.append(oh * lax.rsqrt(ms + NORM_EPS) * a_ref[0, rs, hs].astype(F32)
                      + b_ref[0, rs, hs].astype(F32))
        y = jnp.concatenate(ys, axis=-1).astype(BF16)
        h1 = _stream_rows(h_refs, first_tile, rs) + _dot(y, wout_ref[...])

        un = _rms(h1, gain_ref[...]).astype(BF16)
        cw = cw_ref[...]
        slabs = [slice(lo, hi) for lo, hi in zip(FF_SPLITS[:-1], FF_SPLITS[1:])]

        def shifted(cs):
            return slice(D_FF + cs.start, D_FF + cs.stop)

        def up(cs):
            cg[data, cs] = _dot(un, wup_ref[:, cs])
            cv[data, cs] = _dot(un, wup_ref[:, shifted(cs)])

        for cs in slabs[:2]:
            up(cs)
        h2 = h1
        for s, cs in enumerate(slabs):
            half_gate = _causal_conv(cg, n, cs, cw[:, cs], FFN_CONV)
            val = _causal_conv(cv, n, cs, cw[:, shifted(cs)], FFN_CONV)
            act = (_silu_of_twice(half_gate) * val).astype(BF16)
            if s + 2 < len(slabs):
                up(slabs[s + 2])
            h2 = h2 + _dot(act, wdown_ref[cs, :])
        if final:
            h2 = _rms(h2, gfin_ref[...])
        out_ref[0, rs, :] = h2

    _first_tile_split(i, body, zero_fill)


def _resident(shape):
    nd = len(shape)
    return pl.BlockSpec(shape, lambda b, i: (0,) * nd, pipeline_mode=pl.Buffered(1))


def _tile(width):
    return pl.BlockSpec((1, TM, width), lambda b, i: (b, i, 0))


def _params():
    return pltpu.CompilerParams(dimension_semantics=("arbitrary", "arbitrary"),
                                vmem_limit_bytes=VMEM_LIMIT)


def _token_tile():
    return pl.BlockSpec((1, TM, D_MODEL), lambda b, i: (b, jnp.maximum(i - 1, 0), 0))


def _stream_specs(stream):
    if len(stream) == 1:
        return [_tile(D_MODEL)]
    return [_token_tile(), _resident(stream[1].shape)]


def _layer_resident(stacked, layer):
    rest = stacked.shape[1:]
    return pl.BlockSpec((None,) + rest, lambda b, i: (layer,) + (0,) * len(rest),
                        pipeline_mode=pl.Buffered(1))


def _front(stream, t, layer, stacked, tri):
    bsz = stream[0].shape[0]
    nt = t // TM
    nc = TM // CHUNK
    act = jax.ShapeDtypeStruct((bsz, t, D_MODEL), BF16)
    return pl.pallas_call(
        functools.partial(_front_kernel, from_tokens=len(stream) == 2),
        grid=(bsz, nt),
        in_specs=_stream_specs(stream) + [_layer_resident(s, layer) for s in stacked]
        + [_resident(tri.shape)],
        out_specs=[_tile(D_MODEL)] * 5 + [
            _tile(LANES),
            pl.BlockSpec((1, nc, 2, CAT), lambda b, i: (b, i, 0, 0))],
        out_shape=[act] * 5 + [
            jax.ShapeDtypeStruct((bsz, t, LANES), F32),
            jax.ShapeDtypeStruct((bsz, t // CHUNK, 2, CAT), F32)],
        scratch_shapes=[pltpu.VMEM((SUBLANES + TM, D_MODEL), F32)] * 3
        + [pltpu.VMEM((POOL_CARRY, POOL_WIDTH), F32)]
        + [pltpu.VMEM((TM, D_MODEL), F32)] * 3
        + [pltpu.VMEM((TM, POOL_WIDTH + LANES), F32)],
        compiler_params=_params(),
        name="front",
    )(*stream, *stacked, tri)


def _delta(q, k, v, gcol, grow):
    bsz, t, _ = q.shape
    nt = t // TM
    nc = TM // CHUNK
    def both(width):
        return pl.BlockSpec((bsz, TM, width), lambda i: (0, i, 0))

    chunk_mats = pltpu.VMEM((bsz, nc, CHUNK, CAT), BF16)
    return pl.pallas_call(
        _delta_kernel,
        grid=(nt,),
        in_specs=[both(D_MODEL)] * 3 + [
            both(LANES),
            pl.BlockSpec((bsz, nc, 2, CAT), lambda i: (0, i, 0, 0))],
        out_specs=both(D_MODEL),
        out_shape=jax.ShapeDtypeStruct((bsz, t, D_MODEL), F32),
        scratch_shapes=[pltpu.VMEM((bsz, HEADS // 2, HEAD_DIM, 2 * HEAD_DIM), F32)]
        + [chunk_mats] * 4
        + [pltpu.VMEM((bsz, TM, D_MODEL), BF16),
           pltpu.VMEM((bsz, nc, SUBLANES, D_MODEL), F32)],
        compiler_params=pltpu.CompilerParams(dimension_semantics=("arbitrary",),
                                             vmem_limit_bytes=VMEM_LIMIT),
        name="delta",
    )(q, k, v, gcol, grow)


def _back(o, a, b, stream, layer, stacked, gfin, final):
    bsz, t, _ = o.shape
    nt = t // TM
    if final:
        out_spec, out_len = _token_tile(), t - TM
    else:
        out_spec, out_len = _tile(D_MODEL), t
    return pl.pallas_call(
        functools.partial(_back_kernel, from_tokens=len(stream) == 2, final=final),
        grid=(bsz, nt),
        in_specs=[_tile(D_MODEL)] * 3 + _stream_specs(stream)
        + [_layer_resident(s, layer) for s in stacked] + [_resident(gfin.shape)],
        out_specs=out_spec,
        out_shape=jax.ShapeDtypeStruct((bsz, out_len, D_MODEL), F32),
        scratch_shapes=[pltpu.VMEM((SUBLANES + TM, D_FF), F32)] * 2,
        compiler_params=_params(),
        name="back",
    )(o, a, b, *stream, *stacked, gfin)


def _block_tri():
    r = jnp.arange(TM)
    same_chunk = (r[:, None] // CHUNK) == (r[None, :] // CHUNK)
    return (same_chunk & (r[:, None] >= r[None, :])).astype(BF16)


def _lane_vecs(vals, offset):
    depth, n = vals.shape
    return jnp.zeros((depth, 1, LANES), F32).at[:, 0, offset:offset + n].set(vals.astype(F32))


def _pack_w_in(w_in):
    depth, d_in, in_dim = w_in.shape
    rows = PACK_ROWS
    return pl.pallas_call(
        _pack_w_in_kernel,
        grid=(depth, d_in // rows),
        in_specs=[pl.BlockSpec((None, rows, in_dim), lambda l, r: (l, r, 0))],
        out_specs=pl.BlockSpec((None, rows, in_dim + BA_PAD), lambda l, r: (l, r, 0)),
        out_shape=jax.ShapeDtypeStruct((depth, d_in, in_dim + BA_PAD), BF16),
        compiler_params=_params(),
        name="pack_w_in",
    )(w_in)


def _pack_w_in_kernel(w_ref, o_ref):
    split = QKV_DIM + D_MODEL
    ungated = LANES + POOL_WIDTH
    o_ref[:, :QKV_DIM] = w_ref[:, :QKV_DIM]
    o_ref[:, QKV_DIM:split] = (0.5 * w_ref[:, QKV_DIM:split].astype(F32)).astype(BF16)
    tail = w_ref[:, split:].astype(F32)
    shifted = jnp.concatenate([jnp.zeros((tail.shape[0], BA_PAD), F32), tail], axis=1)
    o_ref[:, split:split + ungated] = shifted[:, :ungated].astype(BF16)
    o_ref[:, split + ungated:] = (0.5 * shifted[:, ungated:]).astype(BF16)


def kernel(x, meta_tokens, norm_mix, w_in, conv_qkv, a_log, dt_bias, head_norm, w_pool,
           pool_scale, w_out, norm_ffn, w_up, conv_ffn, w_down, norm_final):
    bsz, seq, _ = x.shape
    depth = w_in.shape[0]
    t = TM + seq
    first = jnp.concatenate([jnp.zeros((LEAD, D_MODEL), x.dtype), meta_tokens.astype(x.dtype)], axis=0)
    stream = (x, first)
    tri = _block_tri()
    front_params = (
        norm_mix[:, None, :], _pack_w_in(w_in.astype(BF16)), 0.5 * conv_qkv,
        _lane_vecs(a_log, DECAY_LANE), _lane_vecs(dt_bias, DECAY_LANE),
        0.5 * jnp.tile(head_norm, (1, HEADS))[:, None, :], w_pool.astype(BF16),
        0.5 * pool_scale[:, None, :])
    ffn_taps = jnp.concatenate([0.5 * conv_ffn[..., :D_FF], conv_ffn[..., D_FF:]], axis=-1)
    back_params = (w_out.astype(BF16), norm_ffn[:, None, :], w_up.astype(BF16), ffn_taps,
                   w_down.astype(BF16))
    for layer in range(depth):
        q, k, v, a, b, gcol, grow = _front(stream, t, layer, front_params, tri)
        o = _delta(q, k, v, gcol, grow)
        h = _back(o, a, b, stream, layer, back_params, norm_final[None, :], layer == depth - 1)
        stream = (h,)
    return h
```

```python
import functools

import jax
import jax.numpy as jnp
from jax import lax
from jax.experimental import pallas as pl
from jax.experimental.pallas import tpu as pltpu

D_MODEL = 1024
N_META = 16
HEADS = 8
HEAD_DIM = 128
QKV_DIM = 3 * D_MODEL
DN_CONV = 4
CHUNK = 64
POOL_WINDOWS = (2, 4, 8, 16)
POOL_GROUP_DIM = 128
POOL_WIDTH = 512
POOL_OUT_GROUP = 256
D_FF = 2816
FFN_CONV = 3
NORM_EPS = 1e-6

TM = 512
LEAD = TM - N_META
ZERO_ROWS = TM - CHUNK
LANES = 128
SUBLANES = 8
POOL_CARRY = 16
GROUP = 4
CAT = HEADS * CHUNK
MXU_TILE = 256
PACK_ROWS = 256
FF_SPLITS = tuple(range(0, D_FF, 4 * MXU_TILE)) + (D_FF,)
CHUNK_SHIFT = CHUNK.bit_length() - 1
VMEM_LIMIT = 56 * 1024 * 1024

BA_PAD = LANES - 2 * HEADS
BETA_LANE = BA_PAD
DECAY_LANE = BA_PAD + HEADS
_SECTIONS = (("q", D_MODEL), ("k", D_MODEL), ("v", D_MODEL), ("z", D_MODEL),
             ("ba_pool", LANES + POOL_WIDTH), ("gate_a", D_MODEL), ("gate_b", D_MODEL))
W_IN_COLS = {}
_at = 0
for _name, _width in _SECTIONS:
    W_IN_COLS[_name] = (_at, _at + _width)
    _at += _width

F32 = jnp.float32
BF16 = jnp.bfloat16


def _sigmoid(x):
    return 0.5 * jnp.tanh(0.5 * x) + 0.5


def _silu_of_twice(half):
    return half + half * jnp.tanh(half)


def _softplus(x):
    return jnp.maximum(x, 0.0) + jnp.log(1.0 + jnp.exp(-jnp.abs(x)))


def _dot(a, b):
    return jnp.dot(a, b, preferred_element_type=F32)


def _dot_nt(a, b):
    return lax.dot_general(a, b, (((1,), (1,)), ((), ())), preferred_element_type=F32)


def _dot_tn(a, b):
    return lax.dot_general(a, b, (((0,), (0,)), ((), ())), preferred_element_type=F32)


def _rms(x, gain):
    ms = jnp.mean(x * x, axis=-1, keepdims=True)
    return x * lax.rsqrt(ms + NORM_EPS) * gain


def _causal_conv(buf_ref, rows, cs, cw, width):
    groups = rows // SUBLANES
    pre = buf_ref[SUBLANES:SUBLANES + rows, cs]
    cols = pre.shape[1]
    cur = pre.reshape(groups, SUBLANES, cols)
    prev = buf_ref[0:rows, cs].reshape(groups, SUBLANES, cols)
    sub = lax.broadcasted_iota(jnp.int32, (1, SUBLANES, cols), 1)
    acc = pre * cw[width - 1:width, :]
    for j in range(1, width):
        picked = jnp.where(sub < SUBLANES - j, cur, prev)
        shifted = pltpu.roll(picked, j, axis=1).reshape(rows, cols)
        acc = acc + shifted * cw[width - 1 - j:width - j, :]
    buf_ref[0:SUBLANES, cs] = pre[rows - SUBLANES:, :]
    return acc


def _split_stream(refs, from_tokens):
    n = 2 if from_tokens else 1
    return refs[:n], refs[n:]


def _stream_rows(h_refs, first_tile, rs):
    if len(h_refs) == 1:
        return h_refs[0][0, rs, :]
    tok_ref, first_ref = h_refs
    return first_ref[rs, :] if first_tile else tok_ref[0, rs, :]


def _first_tile_split(i, body, zero_fill):
    @pl.when(i == 0)
    def _():
        zero_fill(ZERO_ROWS)
        body(True, ZERO_ROWS, TM - ZERO_ROWS)

    @pl.when(i > 0)
    def _():
        body(False, 0, TM)


def _front_kernel(*refs, from_tokens):
    h_refs, refs = _split_stream(refs, from_tokens)
    (gain_ref, w_ref, cw_ref, avec_ref, dtvec_ref, hn_ref, wpool_ref, ps_ref, tri_ref,
     q_out, k_out, v_out, a_out, b_out, gcol_out, grow_out,
     cq, ck, cv, cp, zbuf, gabuf, gbbuf, pbbuf) = refs
    i = pl.program_id(1)

    def zero_fill(rows):
        zeros = jnp.zeros((SUBLANES, D_MODEL), F32)
        cq[0:SUBLANES, :] = zeros
        ck[0:SUBLANES, :] = zeros
        cv[0:SUBLANES, :] = zeros
        cp[...] = jnp.zeros_like(cp)
        for out in (q_out, k_out, v_out, a_out, b_out, gcol_out):
            out[0, 0:rows, :] = jnp.zeros((rows,) + out.shape[2:], out.dtype)
        nchunk = rows // CHUNK
        grow_out[0, 0:nchunk] = jnp.zeros((nchunk,) + grow_out.shape[2:], F32)

    def body(first_tile, r0, n):
        rs = slice(r0, r0 + n)
        data = slice(SUBLANES, SUBLANES + n)
        xn = _rms(_stream_rows(h_refs, first_tile, rs), gain_ref[...]).astype(BF16)
        cw = cw_ref[...]
        full = slice(0, D_MODEL)

        def proj(name):
            lo, hi = W_IN_COLS[name]
            return _dot(xn, w_ref[:, lo:hi])

        cq[data, :] = proj("q")
        ck[data, :] = proj("k")
        cv[data, :] = proj("v")
        gabuf[0:n, :] = proj("gate_a")
        zbuf[0:n, :] = proj("z")
        pbbuf[0:n, :] = proj("ba_pool")
        gbbuf[0:n, :] = proj("gate_b")

        def qkv_section(buf_ref, sec):
            return _silu_of_twice(
                _causal_conv(buf_ref, n, full, cw[:, sec * D_MODEL:(sec + 1) * D_MODEL], DN_CONV))

        def l2norm_store(xs, out_ref, scale):
            for h in range(HEADS):
                hs = slice(h * HEAD_DIM, (h + 1) * HEAD_DIM)
                xh = xs[:, hs]
                ss = jnp.sum(xh * xh, axis=-1, keepdims=True)
                out_ref[0, rs, hs] = (xh * (lax.rsqrt(ss + NORM_EPS) * scale)).astype(BF16)

        l2norm_store(qkv_section(cq, 0), q_out, HEAD_DIM ** -0.5)
        l2norm_store(qkv_section(ck, 1), k_out, 1.0)
        v_out[0, rs, :] = qkv_section(cv, 2).astype(BF16)
        a_out[0, rs, :] = (hn_ref[...] * _silu_of_twice(zbuf[0:n, :])
                           * (1.0 + jnp.tanh(gabuf[0:n, :]))).astype(BF16)

        ba = pbbuf[0:n, :LANES]
        p = pbbuf[0:n, LANES:]
        row = lax.broadcasted_iota(jnp.int32, (n, 1), 0)
        pos = i * TM + r0 + row - LEAD
        ys = []
        for gi, win in enumerate(POOL_WINDOWS):
            gs = slice(gi * POOL_GROUP_DIM, (gi + 1) * POOL_GROUP_DIM)
            pg = p[:, gs]
            s = jnp.concatenate([cp[:, gs], pg], axis=0)
            sh = 1
            while sh < win:
                s = s + pltpu.roll(s, sh, axis=0)
                sh *= 2
            cnt = jnp.clip(pos + 1, 1, win).astype(F32)
            pooled = s[POOL_CARRY:, :] / cnt - pg
            ys.append(_dot(pooled.astype(BF16), wpool_ref[gi]))
        cp[...] = p[n - POOL_CARRY:, :]
        yb = jnp.concatenate(ys, axis=-1)
        b_out[0, rs, :] = ((1.0 + jnp.tanh(gbbuf[0:n, :])) * yb * ps_ref[...]).astype(BF16)

        lane = lax.broadcasted_iota(jnp.int32, (1, LANES), 1)
        valid = pos >= 0
        beta = jnp.where(valid, _sigmoid(ba), 0.0)
        g = jnp.where(valid, -jnp.exp(avec_ref[...]) * _softplus(ba + dtvec_ref[...]), 0.0)
        g = jnp.where(lane >= DECAY_LANE, g, 0.0)
        g_hi = g.astype(BF16)
        g_lo = (g - g_hi.astype(F32)).astype(BF16)
        tri = tri_ref[0:n, 0:n]
        gc = _dot(tri, g_hi) + _dot(tri, g_lo)
        col = jnp.where(lane < DECAY_LANE, beta, gc)
        gcol_out[0, rs, :] = col
        rowform = col.T
        for c in range(n // CHUNK):
            cs = slice(c * CHUNK, (c + 1) * CHUNK)
            grow_out[0, r0 // CHUNK + c] = jnp.concatenate(
                [jnp.concatenate([rowform[l0 + h:l0 + h + 1, cs] for h in range(HEADS)], axis=1)
                 for l0 in (BETA_LANE, DECAY_LANE)], axis=0)

    _first_tile_split(i, body, zero_fill)


def _pair_blocks(x):
    a, b = x[:, :HEAD_DIM], x[:, HEAD_DIM:]
    z = jnp.zeros_like(a)
    return jnp.concatenate([jnp.concatenate([a, z], axis=1), jnp.concatenate([z, b], axis=1)], axis=0)


def _delta_kernel(q_ref, k_ref, v_ref, gcol_ref, grow_ref, o_ref,
                  state, tb_s, tbe_s, qk_s, deg_s, kdec_s, gtot_s):
    i = pl.program_id(0)
    nb = q_ref.shape[0]
    gw = GROUP * CHUNK

    ri = lax.broadcasted_iota(jnp.int32, (CHUNK, gw), 0)
    ci = lax.broadcasted_iota(jnp.int32, (CHUNK, gw), 1) & (CHUNK - 1)
    causal = ri >= ci
    strict = ri > ci
    eye = ri == ci
    bd_mask = ((lax.broadcasted_iota(jnp.int32, (gw, gw), 0) >> CHUNK_SHIFT)
               == (lax.broadcasted_iota(jnp.int32, (gw, gw), 1) >> CHUNK_SHIFT))
    lane_half = lax.broadcasted_iota(jnp.int32, (CHUNK, LANES), 1) >> CHUNK_SHIFT
    zblk = jnp.zeros((CHUNK, HEAD_DIM), BF16)

    def block_diag(pb):
        return jnp.where(bd_mask, jnp.concatenate([pb] * GROUP, axis=0), jnp.zeros((), BF16))

    def lane_bcast(col, idx):
        return jnp.take_along_axis(col, idx, axis=1)

    def phase_a(chunks):
        xs, ps, meta = [], [], []
        for c in chunks:
            rows = slice(c * CHUNK, (c + 1) * CHUNK)
            for b in range(nb):
                col = gcol_ref[b, rows, :]
                rowc = grow_ref[b, c]
                beta_r, gc_r = rowc[0:1, :], rowc[1:2, :]
                beta_c = jnp.concatenate(
                    [lane_bcast(col, BETA_LANE + 2 * p + lane_half) for p in range(HEADS // 2)], axis=1)
                gc_c = jnp.concatenate(
                    [lane_bcast(col, DECAY_LANE + 2 * p + lane_half) for p in range(HEADS // 2)], axis=1)
                last = col[CHUNK - 1:CHUNK, :]
                ek_col = jnp.exp(last - col)
                gtot = jnp.exp(last)
                k_all = k_ref[b, rows, :]
                q_all = q_ref[b, rows, :]
                kdec, gts = [], []
                for h in range(HEADS):
                    hs = slice(h * HEAD_DIM, (h + 1) * HEAD_DIM)
                    ek_b = lane_bcast(ek_col, jnp.full((CHUNK, LANES), DECAY_LANE + h, jnp.int32))
                    kdec.append((k_all[:, hs].astype(F32) * ek_b).astype(BF16))
                    gts.append(jnp.broadcast_to(gtot[:, DECAY_LANE + h:DECAY_LANE + h + 1],
                                                (SUBLANES, HEAD_DIM)))
                kdec_s[b, rows, :] = jnp.concatenate(kdec, axis=1)
                gtot_s[b, c] = jnp.concatenate(gts, axis=1)
                for g in range(HEADS // GROUP):
                    gl = slice(g * GROUP * HEAD_DIM, (g + 1) * GROUP * HEAD_DIM)
                    cb = slice(g * gw, (g + 1) * gw)
                    kg = k_all[:, gl]
                    lhs = jnp.concatenate([kg, q_all[:, gl]], axis=0)
                    rhs_t = jnp.concatenate(
                        [jnp.concatenate([kg[:, m * HEAD_DIM:(m + 1) * HEAD_DIM] if m == h else zblk
                                          for m in range(GROUP)], axis=1) for h in range(GROUP)], axis=0)
                    sc = _dot_nt(lhs, rhs_t)
                    decay = jnp.exp(jnp.minimum(gc_c[:, cb] - gc_r[:, cb], 0.0))
                    p0 = jnp.where(strict, -(sc[:CHUNK] * beta_c[:, cb] * decay), 0.0)
                    qk_s[b, c, :, cb] = jnp.where(causal, sc[CHUNK:] * decay, 0.0).astype(BF16)
                    eg_r = jnp.exp(gc_r[:, cb])
                    deg_s[b, c, :, cb] = jnp.where(eye, eg_r, 0.0).astype(BF16)
                    xs.append(jnp.where(eye, 1.0, 0.0) + p0)
                    ps.append(p0)
                    meta.append((b, c, cb, beta_r[:, cb], eg_r))
        for s in range(6):
            for u in range(len(xs)):
                pb = ps[u].astype(BF16)
                bd = block_diag(pb)
                if s == 0:
                    ps[u] = _dot(pb, bd)
                elif s < 5:
                    r = _dot(jnp.concatenate([xs[u].astype(BF16), pb], axis=0), bd)
                    xs[u] = xs[u] + r[:CHUNK]
                    ps[u] = r[CHUNK:]
                else:
                    xs[u] = xs[u] + _dot(xs[u].astype(BF16), bd)
        for u, (b, c, cb, beta_r, eg_r) in enumerate(meta):
            tb = xs[u] * beta_r
            tb_s[b, c, :, cb] = tb.astype(BF16)
            tbe_s[b, c, :, cb] = (-(tb * eg_r)).astype(BF16)

    units = [(b, p) for b in range(nb) for p in range(HEADS // 2)]

    def phase_b(chunks):
        for c in chunks:
            rows = slice(c * CHUNK, (c + 1) * CHUNK)
            r1s, vns = [], []
            for b, p in units:
                pw = slice(p * 2 * HEAD_DIM, (p + 1) * 2 * HEAD_DIM)
                kq = jnp.concatenate([k_ref[b, rows, pw], q_ref[b, rows, pw]], axis=0)
                r1s.append(_dot(kq, _pair_blocks(state[b, p].astype(BF16))))
            for u, (b, p) in enumerate(units):
                pw = slice(p * 2 * HEAD_DIM, (p + 1) * 2 * HEAD_DIM)
                pc = slice(p * 2 * CHUNK, (p + 1) * 2 * CHUNK)
                ks = r1s[u][:CHUNK].astype(BF16)
                rhs = jnp.concatenate([_pair_blocks(v_ref[b, rows, pw]), _pair_blocks(ks)], axis=0)
                lhs = jnp.concatenate([tb_s[b, c, :, pc], tbe_s[b, c, :, pc]], axis=1)
                vns.append(_pair_blocks(_dot(lhs, rhs).astype(BF16)))
            for u, (b, p) in enumerate(units):
                pw = slice(p * 2 * HEAD_DIM, (p + 1) * 2 * HEAD_DIM)
                kd = kdec_s[b, rows, pw]
                ds = _dot_tn(jnp.concatenate([kd[:, :HEAD_DIM], kd[:, HEAD_DIM:]], axis=0), vns[u])
                state[b, p] = state[b, p] * gtot_s[b, c][0:1, pw] + ds
            for u, (b, p) in enumerate(units):
                pw = slice(p * 2 * HEAD_DIM, (p + 1) * 2 * HEAD_DIM)
                pc = slice(p * 2 * CHUNK, (p + 1) * 2 * CHUNK)
                qs = r1s[u][CHUNK:].astype(BF16)
                rhs = jnp.concatenate([vns[u], _pair_blocks(qs)], axis=0)
                lhs = jnp.concatenate([qk_s[b, c, :, pc], deg_s[b, c, :, pc]], axis=1)
                o_ref[b, rows, pw] = _dot(lhs, rhs)

    all_chunks = list(range(TM // CHUNK))

    @pl.when(i == 0)
    def _():
        state[...] = jnp.zeros_like(state)
        o_ref[:, 0:ZERO_ROWS, :] = jnp.zeros((nb, ZERO_ROWS, D_MODEL), o_ref.dtype)
        phase_a(all_chunks[-1:])
        phase_b(all_chunks[-1:])

    @pl.when(i > 0)
    def _():
        phase_a(all_chunks)
        phase_b(all_chunks)


def _back_kernel(o_ref, a_ref, b_ref, *refs, from_tokens, final):
    h_refs, refs = _split_stream(refs, from_tokens)
    (wout_ref, gain_ref, wup_ref, cw_ref, wdown_ref, gfin_ref, out_ref, cg, cv) = refs
    i = pl.program_id(1)

    def zero_fill(rows):
        zeros = jnp.zeros((SUBLANES, D_FF), F32)
        cg[0:SUBLANES, :] = zeros
        cv[0:SUBLANES, :] = zeros
        out_ref[0, 0:rows, :] = jnp.zeros((rows, D_MODEL), out_ref.dtype)

    def body(first_tile, r0, n):
        rs = slice(r0, r0 + n)
        data = slice(SUBLANES, SUBLANES + n)
        ys = []
        for h in range(HEADS):
            hs = slice(h * HEAD_DIM, (h + 1) * HEAD_DIM)
            oh = o_ref[0, rs, hs]
            ms = jnp.mean(oh * oh, axis=-1, keepdims=True)
            ys.append(oh * lax.rsqrt(ms + NORM_EPS) * a_ref[0, rs, hs].astype(F32)
                      + b_ref[0, rs, hs].astype(F32))
        y = jnp.concatenate(ys, axis=-1).astype(BF16)
        h1 = _stream_rows(h_refs, first_tile, rs) + _dot(y, wout_ref[...])

        un = _rms(h1, gain_ref[...]).astype(BF16)
        cw = cw_ref[...]
        slabs = [slice(lo, hi) for lo, hi in zip(FF_SPLITS[:-1], FF_SPLITS[1:])]

        def shifted(cs):
            return slice(D_FF + cs.start, D_FF + cs.stop)

        def up(cs):
            cg[data, cs] = _dot(un, wup_ref[:, cs])
            cv[data, cs] = _dot(un, wup_ref[:, shifted(cs)])

        for cs in slabs[:2]:
            up(cs)
        h2 = h1
        for s, cs in enumerate(slabs):
            half_gate = _causal_conv(cg, n, cs, cw[:, cs], FFN_CONV)
            val = _causal_conv(cv, n, cs, cw[:, shifted(cs)], FFN_CONV)
            act = (_silu_of_twice(half_gate) * val).astype(BF16)
            if s + 2 < len(slabs):
                up(slabs[s + 2])
            h2 = h2 + _dot(act, wdown_ref[cs, :])
        if final:
            h2 = _rms(h2, gfin_ref[...])
        out_ref[0, rs, :] = h2

    _first_tile_split(i, body, zero_fill)


def _resident(shape):
    nd = len(shape)
    return pl.BlockSpec(shape, lambda b, i: (0,) * nd, pipeline_mode=pl.Buffered(1))


def _tile(width):
    return pl.BlockSpec((1, TM, width), lambda b, i: (b, i, 0))


def _params():
    return pltpu.CompilerParams(dimension_semantics=("arbitrary", "arbitrary"),
                                vmem_limit_bytes=VMEM_LIMIT)


def _token_tile():
    return pl.BlockSpec((1, TM, D_MODEL), lambda b, i: (b, jnp.maximum(i - 1, 0), 0))


def _stream_specs(stream):
    if len(stream) == 1:
        return [_tile(D_MODEL)]
    return [_token_tile(), _resident(stream[1].shape)]


def _layer_resident(stacked, layer):
    rest = stacked.shape[1:]
    return pl.BlockSpec((None,) + rest, lambda b, i: (layer,) + (0,) * len(rest),
                        pipeline_mode=pl.Buffered(1))


def _front(stream, t, layer, stacked, tri):
    bsz = stream[0].shape[0]
    nt = t // TM
    nc = TM // CHUNK
    act = jax.ShapeDtypeStruct((bsz, t, D_MODEL), BF16)
    return pl.pallas_call(
        functools.partial(_front_kernel, from_tokens=len(stream) == 2),
        grid=(bsz, nt),
        in_specs=_stream_specs(stream) + [_layer_resident(s, layer) for s in stacked]
        + [_resident(tri.shape)],
        out_specs=[_tile(D_MODEL)] * 5 + [
            _tile(LANES),
            pl.BlockSpec((1, nc, 2, CAT), lambda b, i: (b, i, 0, 0))],
        out_shape=[act] * 5 + [
            jax.ShapeDtypeStruct((bsz, t, LANES), F32),
            jax.ShapeDtypeStruct((bsz, t // CHUNK, 2, CAT), F32)],
        scratch_shapes=[pltpu.VMEM((SUBLANES + TM, D_MODEL), F32)] * 3
        + [pltpu.VMEM((POOL_CARRY, POOL_WIDTH), F32)]
        + [pltpu.VMEM((TM, D_MODEL), F32)] * 3
        + [pltpu.VMEM((TM, POOL_WIDTH + LANES), F32)],
        compiler_params=_params(),
        name="front",
    )(*stream, *stacked, tri)


def _delta(q, k, v, gcol, grow):
    bsz, t, _ = q.shape
    nt = t // TM
    nc = TM // CHUNK
    def both(width):
        return pl.BlockSpec((bsz, TM, width), lambda i: (0, i, 0))

    chunk_mats = pltpu.VMEM((bsz, nc, CHUNK, CAT), BF16)
    return pl.pallas_call(
        _delta_kernel,
        grid=(nt,),
        in_specs=[both(D_MODEL)] * 3 + [
            both(LANES),
            pl.BlockSpec((bsz, nc, 2, CAT), lambda i: (0, i, 0, 0))],
        out_specs=both(D_MODEL),
        out_shape=jax.ShapeDtypeStruct((bsz, t, D_MODEL), F32),
        scratch_shapes=[pltpu.VMEM((bsz, HEADS // 2, HEAD_DIM, 2 * HEAD_DIM), F32)]
        + [chunk_mats] * 4
        + [pltpu.VMEM((bsz, TM, D_MODEL), BF16),
           pltpu.VMEM((bsz, nc, SUBLANES, D_MODEL), F32)],
        compiler_params=pltpu.CompilerParams(dimension_semantics=("arbitrary",),
                                             vmem_limit_bytes=VMEM_LIMIT),
        name="delta",
    )(q, k, v, gcol, grow)


def _back(o, a, b, stream, layer, stacked, gfin, final):
    bsz, t, _ = o.shape
    nt = t // TM
    if final:
        out_spec, out_len = _token_tile(), t - TM
    else:
        out_spec, out_len = _tile(D_MODEL), t
    return pl.pallas_call(
        functools.partial(_back_kernel, from_tokens=len(stream) == 2, final=final),
        grid=(bsz, nt),
        in_specs=[_tile(D_MODEL)] * 3 + _stream_specs(stream)
        + [_layer_resident(s, layer) for s in stacked] + [_resident(gfin.shape)],
        out_specs=out_spec,
        out_shape=jax.ShapeDtypeStruct((bsz, out_len, D_MODEL), F32),
        scratch_shapes=[pltpu.VMEM((SUBLANES + TM, D_FF), F32)] * 2,
        compiler_params=_params(),
        name="back",
    )(o, a, b, *stream, *stacked, gfin)


def _block_tri():
    r = jnp.arange(TM)
    same_chunk = (r[:, None] // CHUNK) == (r[None, :] // CHUNK)
    return (same_chunk & (r[:, None] >= r[None, :])).astype(BF16)


def _lane_vecs(vals, offset):
    depth, n = vals.shape
    return jnp.zeros((depth, 1, LANES), F32).at[:, 0, offset:offset + n].set(vals.astype(F32))


def _pack_w_in(w_in):
    depth, d_in, in_dim = w_in.shape
    rows = PACK_ROWS
    return pl.pallas_call(
        _pack_w_in_kernel,
        grid=(depth, d_in // rows),
        in_specs=[pl.BlockSpec((None, rows, in_dim), lambda l, r: (l, r, 0))],
        out_specs=pl.BlockSpec((None, rows, in_dim + BA_PAD), lambda l, r: (l, r, 0)),
        out_shape=jax.ShapeDtypeStruct((depth, d_in, in_dim + BA_PAD), BF16),
        compiler_params=_params(),
        name="pack_w_in",
    )(w_in)


def _pack_w_in_kernel(w_ref, o_ref):
    split = QKV_DIM + D_MODEL
    ungated = LANES + POOL_WIDTH
    o_ref[:, :QKV_DIM] = w_ref[:, :QKV_DIM].astype(BF16)
    o_ref[:, QKV_DIM:split] = (0.5 * w_ref[:, QKV_DIM:split]).astype(BF16)
    tail = w_ref[:, split:]
    shifted = jnp.concatenate([jnp.zeros((tail.shape[0], BA_PAD), F32), tail], axis=1)
    o_ref[:, split:split + ungated] = shifted[:, :ungated].astype(BF16)
    o_ref[:, split + ungated:] = (0.5 * shifted[:, ungated:]).astype(BF16)


def kernel(x, meta_tokens, norm_mix, w_in, conv_qkv, a_log, dt_bias, head_norm, w_pool,
           pool_scale, w_out, norm_ffn, w_up, conv_ffn, w_down, norm_final):
    bsz, seq, _ = x.shape
    depth = w_in.shape[0]
    t = TM + seq
    first = jnp.concatenate([jnp.zeros((LEAD, D_MODEL), x.dtype), meta_tokens.astype(x.dtype)], axis=0)
    stream = (x, first)
    tri = _block_tri()
    front_params = (
        norm_mix[:, None, :], _pack_w_in(w_in), 0.5 * conv_qkv,
        _lane_vecs(a_log, DECAY_LANE), _lane_vecs(dt_bias, DECAY_LANE),
        0.5 * jnp.tile(head_norm, (1, HEADS))[:, None, :], w_pool.astype(BF16),
        0.5 * pool_scale[:, None, :])
    ffn_taps = jnp.concatenate([0.5 * conv_ffn[..., :D_FF], conv_ffn[..., D_FF:]], axis=-1)
    back_params = (w_out.astype(BF16), norm_ffn[:, None, :], w_up.astype(BF16), ffn_taps,
                   w_down.astype(BF16))
    for layer in range(depth):
        q, k, v, a, b, gcol, grow = _front(stream, t, layer, front_params, tri)
        o = _delta(q, k, v, gcol, grow)
        h = _back(o, a, b, stream, layer, back_params, norm_final[None, :], layer == depth - 1)
        stream = (h,)
    return h
```
